```python
import math
import jax, jax.numpy as jnp
from jax import lax
import numpy as np

D_MODEL = 4096
BATCH = 8
SEQ = 2048
DEPTH = 1
DEC_BATCH = 32
DEC_SEQ = 32
PAST_LEN = 4096

CHUNK = 64
D_ATTN = D_MODEL // 2
HEAD_DIM = 128
N_HEADS_A = D_ATTN // HEAD_DIM
D_SSM = D_MODEL // 2
SSM_GROUP_CH = 16
SSM_GROUPS = D_SSM // SSM_GROUP_CH
SSM_STATE = 64
D_FF = 11008
N_MOD = 9
Q_BLOCK = 128
HALF_STEP = 0.5
EPS = 1e-6
DT_MIN = 1e-3
DT_MAX = 1e-1
FORGET_BIAS_INIT = 2.0
IN_COLS = 3 * D_ATTN + N_HEADS_A + D_SSM

kernel_name = "fox_s5_macaron_adaln_stream"


def rms_norm(x, g):
    x32 = x.astype(jnp.float32)
    y = x32 * lax.rsqrt(jnp.mean(x32 * x32, axis=-1, keepdims=True) + EPS)
    return (y * g.astype(jnp.float32)).astype(x.dtype)


def modulate(x, g, shift, scale):
    return rms_norm(x, g) * (1 + scale) + shift


def swiglu(h, w_in, w_out):
    gate, up = jnp.split(h @ w_in, 2, axis=-1)
    return (jax.nn.silu(gate) * up) @ w_out


def fox_block(q, k, v, fq, fk, q_pos, k_pos):
    s = jnp.einsum('bqhd,bkhd->bhqk', q, k).astype(jnp.float32) * (HEAD_DIM ** -0.5)
    s = s + jnp.transpose(fq, (0, 2, 1))[..., :, None] - jnp.transpose(fk, (0, 2, 1))[..., None, :]
    mask = k_pos[None, :] <= q_pos[:, None]
    s = jnp.where(mask, s, -jnp.inf)
    p = jax.nn.softmax(s, axis=-1).astype(v.dtype)
    return jnp.einsum('bhqk,bkhd->bqhd', p, v)


def fox_prompt(q, k, v, logf):
    B, L = q.shape[:2]
    F = jnp.cumsum(logf.astype(jnp.float32), axis=1)
    nb = L // Q_BLOCK
    qb = q.reshape(B, nb, Q_BLOCK, N_HEADS_A, HEAD_DIM).transpose(1, 0, 2, 3, 4)
    fb = F.reshape(B, nb, Q_BLOCK, N_HEADS_A).transpose(1, 0, 2, 3)
    k_pos = jnp.arange(L)

    def one_block(args):
        qi, fi, i = args
        q_pos = i * Q_BLOCK + jnp.arange(Q_BLOCK)
        return fox_block(qi, k, v, fi, F, q_pos, k_pos)

    o = lax.map(one_block, (qb, fb, jnp.arange(nb)))
    return o.transpose(1, 0, 2, 3, 4).reshape(B, L, D_ATTN)


def fox_sample(q, k, v, logf, ck, cv, clogf):
    B, S = q.shape[:2]
    P = ck.shape[1]
    k_all = jnp.concatenate([ck, k], axis=1)
    v_all = jnp.concatenate([cv, v], axis=1)
    F = jnp.cumsum(jnp.concatenate([clogf.astype(jnp.float32), logf.astype(jnp.float32)], axis=1), axis=1)
    q_pos = P + jnp.arange(S)
    k_pos = jnp.arange(P + S)
    o = fox_block(q, k_all, v_all, F[:, P:], F, q_pos, k_pos)
    return o.reshape(B, S, D_ATTN)


def s5_branch(u, h0, a_re, a_im, log_dt, b_re, b_im, c_re, c_im, d, w_glu):
    f32 = jnp.float32
    B, L = u.shape[:2]
    uf = u.reshape(B, L, SSM_GROUPS, SSM_GROUP_CH).astype(f32)
    a_re = a_re.astype(f32)
    a_im = a_im.astype(f32)
    dt = jnp.exp(log_dt.astype(f32))[:, None]
    mag = jnp.exp(a_re * dt)
    abar_re = mag * jnp.cos(a_im * dt)
    abar_im = mag * jnp.sin(a_im * dt)
    xr = abar_re - 1
    den = a_re * a_re + a_im * a_im
    coef_re = ((xr * a_re + abar_im * a_im) / den)[..., None]
    coef_im = ((abar_im * a_re - xr * a_im) / den)[..., None]
    b_re = b_re.astype(f32)
    b_im = b_im.astype(f32)
    bb_re = coef_re * b_re - coef_im * b_im
    bb_im = coef_re * b_im + coef_im * b_re
    bu_re = jnp.einsum('blgc,gpc->blgp', uf, bb_re)
    bu_im = jnp.einsum('blgc,gpc->blgp', uf, bb_im)
    if h0 is not None:
        h0r = h0[0].astype(f32)
        h0i = h0[1].astype(f32)
        bu_re = bu_re.at[:, 0].add(abar_re * h0r - abar_im * h0i)
        bu_im = bu_im.at[:, 0].add(abar_re * h0i + abar_im * h0r)
    a_seq_re = jnp.broadcast_to(abar_re, (1, L, SSM_GROUPS, SSM_STATE))
    a_seq_im = jnp.broadcast_to(abar_im, (1, L, SSM_GROUPS, SSM_STATE))

    def combine(e1, e2):
        a1r, a1i, b1r, b1i = e1
        a2r, a2i, b2r, b2i = e2
        return (a2r * a1r - a2i * a1i, a2r * a1i + a2i * a1r,
                a2r * b1r - a2i * b1i + b2r, a2r * b1i + a2i * b1r + b2i)

    _, _, hr, hi = lax.associative_scan(combine, (a_seq_re, a_seq_im, bu_re, bu_im), axis=1)
    y = (jnp.einsum('blgp,gcp->blgc', hr, c_re.astype(f32))
         - jnp.einsum('blgp,gcp->blgc', hi, c_im.astype(f32))
         + d.astype(f32) * uf)
    y = jax.nn.gelu(y.reshape(B, L, D_SSM)).astype(u.dtype)
    ya, yg = jnp.split(y @ w_glu, 2, axis=-1)
    return ya * jax.nn.sigmoid(yg), hr[:, -1], hi[:, -1]


def hybrid_layer(x, c, attn_cache, ssm_h0, w_ada, b_ada, norm_g, w_ffn1_in, w_ffn1_out,
                 w_in, b_forget, q_norm_g, k_norm_g, w_attn_out,
                 ssm_a_re, ssm_a_im, ssm_log_dt, ssm_b_re, ssm_b_im, ssm_c_re, ssm_c_im, ssm_d,
                 w_glu, w_gate, b_gate, w_out, w_ffn2_in, w_ffn2_out):
    B, L, _ = x.shape
    mod = (c @ w_ada + b_ada).reshape(B, N_MOD, D_MODEL)[:, :, None, :]
    h = modulate(x, norm_g[0], mod[:, 0], mod[:, 1])
    x = x + HALF_STEP * (1 + mod[:, 2]) * swiglu(h, w_ffn1_in, w_ffn1_out)
    h = modulate(x, norm_g[1], mod[:, 3], mod[:, 4])
    q, k, v, fl, u = jnp.split(h @ w_in, [D_ATTN, 2 * D_ATTN, 3 * D_ATTN, 3 * D_ATTN + N_HEADS_A], axis=-1)
    q = rms_norm(q.reshape(B, L, N_HEADS_A, HEAD_DIM), q_norm_g)
    k = rms_norm(k.reshape(B, L, N_HEADS_A, HEAD_DIM), k_norm_g)
    v = v.reshape(B, L, N_HEADS_A, HEAD_DIM)
    logf = jax.nn.log_sigmoid((fl + b_forget).astype(jnp.float32))
    if attn_cache is None:
        o_attn = fox_prompt(q, k, v, logf)
    else:
        o_attn = fox_sample(q, k, v, logf, attn_cache[0], attn_cache[1], attn_cache[2])
    a_branch = o_attn @ w_attn_out
    s_branch, h_re, h_im = s5_branch(u, ssm_h0, ssm_a_re, ssm_a_im, ssm_log_dt, ssm_b_re, ssm_b_im,
                                     ssm_c_re, ssm_c_im, ssm_d, w_glu)
    g_a, g_s = jnp.split(jax.nn.sigmoid(h @ w_gate + b_gate), 2, axis=-1)
    merged = g_a * a_branch + g_s * s_branch.astype(x.dtype)
    x = x + (1 + mod[:, 5]) * (merged @ w_out)
    h = modulate(x, norm_g[2], mod[:, 6], mod[:, 7])
    x = x + HALF_STEP * (1 + mod[:, 8]) * swiglu(h, w_ffn2_in, w_ffn2_out)
    return x, k, v, logf, h_re, h_im


def setup_inputs(seed: int = 0) -> dict:
    key = jax.random.key(seed)
    ks = jax.random.split(key, 40)
    f32 = jnp.float32

    def nrm(k, shape, scale):
        return jax.random.normal(k, shape, f32) * scale

    return {
        "x_prompt": nrm(ks[0], (BATCH, SEQ, D_MODEL), 1.0),
        "x_sample": nrm(ks[1], (DEC_BATCH, DEC_SEQ, D_MODEL), 1.0),
        "cache_k": nrm(ks[2], (DEPTH, DEC_BATCH, PAST_LEN, N_HEADS_A, HEAD_DIM), 1.0),
        "cache_v": nrm(ks[3], (DEPTH, DEC_BATCH, PAST_LEN, N_HEADS_A, HEAD_DIM), 1.0),
        "cache_logf": jax.nn.log_sigmoid(FORGET_BIAS_INIT + nrm(ks[4], (DEPTH, DEC_BATCH, PAST_LEN, N_HEADS_A), 1.0)),
        "state_ssm_re": nrm(ks[5], (DEPTH, DEC_BATCH, SSM_GROUPS, SSM_STATE), 0.1),
        "state_ssm_im": nrm(ks[6], (DEPTH, DEC_BATCH, SSM_GROUPS, SSM_STATE), 0.1),
        "c_prompt": nrm(ks[7], (BATCH, D_MODEL), 1.0),
        "c_sample": nrm(ks[8], (DEC_BATCH, D_MODEL), 1.0),
        "w_ada": nrm(ks[9], (DEPTH, D_MODEL, N_MOD * D_MODEL), 0.1 * D_MODEL ** -0.5),
        "b_ada": nrm(ks[10], (DEPTH, N_MOD * D_MODEL), 0.01),
        "norm_g": 1.0 + nrm(ks[11], (DEPTH, 3, D_MODEL), 0.01),
        "w_ffn1_in": nrm(ks[12], (DEPTH, D_MODEL, 2 * D_FF), D_MODEL ** -0.5),
        "w_ffn1_out": nrm(ks[13], (DEPTH, D_FF, D_MODEL), D_FF ** -0.5),
        "w_in": nrm(ks[14], (DEPTH, D_MODEL, IN_COLS), D_MODEL ** -0.5),
        "b_forget": FORGET_BIAS_INIT + nrm(ks[15], (DEPTH, N_HEADS_A), 0.1),
        "q_norm_g": 1.0 + nrm(ks[16], (DEPTH, HEAD_DIM), 0.01),
        "k_norm_g": 1.0 + nrm(ks[17], (DEPTH, HEAD_DIM), 0.01),
        "w_attn_out": nrm(ks[18], (DEPTH, D_ATTN, D_MODEL), D_ATTN ** -0.5),
        "ssm_a_re": -0.5 + nrm(ks[19], (DEPTH, SSM_GROUPS, SSM_STATE), 0.01),
        "ssm_a_im": math.pi * jnp.arange(SSM_STATE, dtype=f32) + nrm(ks[20], (DEPTH, SSM_GROUPS, SSM_STATE), 0.01),
        "ssm_log_dt": jax.random.uniform(ks[21], (DEPTH, SSM_GROUPS), f32, math.log(DT_MIN), math.log(DT_MAX)),
        "ssm_b_re": nrm(ks[22], (DEPTH, SSM_GROUPS, SSM_STATE, SSM_GROUP_CH), (2 * SSM_GROUP_CH) ** -0.5),
        "ssm_b_im": nrm(ks[23], (DEPTH, SSM_GROUPS, SSM_STATE, SSM_GROUP_CH), (2 * SSM_GROUP_CH) ** -0.5),
        "ssm_c_re": nrm(ks[24], (DEPTH, SSM_GROUPS, SSM_GROUP_CH, SSM_STATE), SSM_STATE ** -0.5),
        "ssm_c_im": nrm(ks[25], (DEPTH, SSM_GROUPS, SSM_GROUP_CH, SSM_STATE), SSM_STATE ** -0.5),
        "ssm_d": nrm(ks[26], (DEPTH, SSM_GROUPS, SSM_GROUP_CH), 1.0),
        "w_glu": nrm(ks[27], (DEPTH, D_SSM, 2 * D_MODEL), D_SSM ** -0.5),
        "w_gate": nrm(ks[28], (DEPTH, D_MODEL, 2 * D_MODEL), D_MODEL ** -0.5),
        "b_gate": nrm(ks[29], (DEPTH, 2 * D_MODEL), 0.01),
        "w_out": nrm(ks[30], (DEPTH, D_MODEL, D_MODEL), D_MODEL ** -0.5),
        "w_ffn2_in": nrm(ks[31], (DEPTH, D_MODEL, 2 * D_FF), D_MODEL ** -0.5),
        "w_ffn2_out": nrm(ks[32], (DEPTH, D_FF, D_MODEL), D_FF ** -0.5),
    }


def reference(x_prompt, x_sample, cache_k, cache_v, cache_logf, state_ssm_re, state_ssm_im,
              c_prompt, c_sample, w_ada, b_ada, norm_g, w_ffn1_in, w_ffn1_out, w_in, b_forget,
              q_norm_g, k_norm_g, w_attn_out, ssm_a_re, ssm_a_im, ssm_log_dt, ssm_b_re, ssm_b_im,
              ssm_c_re, ssm_c_im, ssm_d, w_glu, w_gate, b_gate, w_out, w_ffn2_in, w_ffn2_out):
    xp = x_prompt
    xs = x_sample
    kp, vp, fp, srp, sip = [], [], [], [], []
    ksm, vsm, fsm, srs, sis = [], [], [], [], []
    for l in range(DEPTH):
        lw = (w_ada[l], b_ada[l], norm_g[l], w_ffn1_in[l], w_ffn1_out[l], w_in[l], b_forget[l],
              q_norm_g[l], k_norm_g[l], w_attn_out[l], ssm_a_re[l], ssm_a_im[l], ssm_log_dt[l],
              ssm_b_re[l], ssm_b_im[l], ssm_c_re[l], ssm_c_im[l], ssm_d[l], w_glu[l], w_gate[l],
              b_gate[l], w_out[l], w_ffn2_in[l], w_ffn2_out[l])
        xp, k1, v1, f1, r1, i1 = hybrid_layer(xp, c_prompt, None, None, *lw)
        xs, k2, v2, f2, r2, i2 = hybrid_layer(xs, c_sample, (cache_k[l], cache_v[l], cache_logf[l]),
                                              (state_ssm_re[l], state_ssm_im[l]), *lw)
        kp.append(k1); vp.append(v1); fp.append(f1); srp.append(r1); sip.append(i1)
        ksm.append(k2); vsm.append(v2); fsm.append(f2); srs.append(r2); sis.append(i2)
    return (xp, xs,
            jnp.stack(kp), jnp.stack(vp), jnp.stack(fp), jnp.stack(srp), jnp.stack(sip),
            jnp.stack(ksm), jnp.stack(vsm), jnp.stack(fsm), jnp.stack(srs), jnp.stack(sis))
```

```python
import functools
import math

import jax
import jax.numpy as jnp
from jax import lax
from jax.experimental import pallas as pl
from jax.experimental.pallas import tpu as pltpu

F32 = jnp.float32
BF16 = jnp.bfloat16

EPS = 1e-6
HALF_STEP = 0.5
N_MOD = 9
NEG_BIG = -1e30
LANES = 128
SUBLANES = 8
MIB = 1024 * 1024
CUMSUM_CHUNK = 256
S5_GROUPS_PER_TILE = 8


def _params(n_axes, vmem_mib):
    return pltpu.CompilerParams(
        dimension_semantics=("arbitrary",) * n_axes,
        vmem_limit_bytes=vmem_mib * MIB,
    )


def _dot(a, b):
    return jnp.dot(a, b, preferred_element_type=F32)


def _dot_nt(a, b):
    return lax.dot_general(a, b, (((1,), (1,)), ((), ())), preferred_element_type=F32)


def _row_periodic(acc, per, fn):
    m, n = acc.shape
    return fn(acc.reshape(m // per, per, n)).reshape(m, n)


def _mm_call(name, m, n, bm, bn, lhs, rhs, extras, out_dtypes, body, vmem_mib=48):
    assert m % bm == 0 and n % bn == 0, (name, m, n, bm, bn)
    in_specs = []
    for a in lhs:
        in_specs.append(pl.BlockSpec((bm, a.shape[1]), lambda i, j: (i, 0)))
    for w, off in rhs:
        in_specs.append(pl.BlockSpec((w.shape[0], bn), lambda i, j, off=off: (0, j + off)))
    for _, bs, im in extras:
        in_specs.append(pl.BlockSpec(bs, im))
    out_specs = [pl.BlockSpec((bm, bn), lambda i, j: (i, j)) for _ in out_dtypes]
    out_shape = [jax.ShapeDtypeStruct((m, n), dt) for dt in out_dtypes]
    nl, nr, ne = len(lhs), len(rhs), len(extras)

    def kern(*refs):
        body(refs[:nl], refs[nl:nl + nr], refs[nl + nr:nl + nr + ne], refs[nl + nr + ne:])

    outs = pl.pallas_call(
        kern,
        out_shape=out_shape,
        grid=(m // bm, n // bn),
        in_specs=in_specs,
        out_specs=out_specs,
        compiler_params=_params(2, vmem_mib),
        name=name,
    )(*lhs, *[w for w, _ in rhs], *[e for e, _, _ in extras])
    return outs


def _pick(total, prefs):
    for p in prefs:
        if total % p == 0:
            return p
    return total


def _ada_call(c, w_ada, b_ada):
    mb, d = c.shape
    n = w_ada.shape[1]
    bn = _pick(n, (512, 256, 128))

    def kern(c_ref, w_ref, b_ref, o_ref):
        o_ref[...] = _dot(c_ref[...].astype(BF16), w_ref[...].astype(BF16)) + b_ref[...]

    return pl.pallas_call(
        kern,
        out_shape=jax.ShapeDtypeStruct((mb, n), F32),
        grid=(n // bn,),
        in_specs=[pl.BlockSpec((mb, d), lambda j: (0, 0)),
                  pl.BlockSpec((d, bn), lambda j: (0, j)),
                  pl.BlockSpec((1, bn), lambda j: (0, j))],
        out_specs=pl.BlockSpec((mb, bn), lambda j: (0, j)),
        compiler_params=_params(1, 40),
        name="ada_mod",
    )(c, w_ada, b_ada.reshape(1, n))


def _norm_mod_call(x, g, scale, shift, per, rows_per_group):
    m, d = x.shape
    bm = _pick(m, (256, 128, 64, 32, 8))
    bm = min(bm, rows_per_group)

    def kern(x_ref, g_ref, sc_ref, sh_ref, o_ref):
        xv = x_ref[...]
        ms = jnp.mean(xv * xv, axis=-1, keepdims=True)
        y = xv * lax.rsqrt(ms + EPS) * g_ref[...]
        y = _row_periodic(y, per, lambda y3: y3 * (1.0 + sc_ref[...]) + sh_ref[...])
        o_ref[...] = y.astype(BF16)

    grp = lambda i: (i * bm // rows_per_group, 0, 0)
    return pl.pallas_call(
        kern,
        out_shape=jax.ShapeDtypeStruct((m, d), BF16),
        grid=(m // bm,),
        in_specs=[pl.BlockSpec((bm, d), lambda i: (i, 0)),
                  pl.BlockSpec((1, d), lambda i: (0, 0)),
                  pl.BlockSpec((1, per, d), grp),
                  pl.BlockSpec((1, per, d), grp)],
        out_specs=pl.BlockSpec((bm, d), lambda i: (i, 0)),
        compiler_params=_params(1, 40),
        name="norm_mod",
    )(x, g.reshape(1, d), scale, shift)


def _ffn_in_call(h, w_in_bf):
    m, d = h.shape
    f = w_in_bf.shape[1] // 2
    bm = _pick(m, (1024, 512, 256, 128, 64, 32, 8))
    bn = _pick(f, (256, 128))

    def body(a, w, e, o):
        hv = a[0][...]
        gate = _dot(hv, w[0][...])
        up = _dot(hv, w[1][...])
        o[0][...] = (gate * jax.nn.sigmoid(gate) * up).astype(BF16)

    (act,) = _mm_call("ffn_in", m, f, bm, bn, [h], [(w_in_bf, 0), (w_in_bf, f // bn)], [],
                      [BF16], body, vmem_mib=48)
    return act


def _mm_res_call(name, a, w_bf, x, gate, coef, per, rows_per_group):
    m, k = a.shape
    n = w_bf.shape[1]
    bm = _pick(m, (512, 256, 128, 64, 32, 8))
    bm = min(bm, rows_per_group)
    bn = _pick(n, (256, 128))

    def body(lhs, rhs, e, o):
        acc = _dot(lhs[0][...], rhs[0][...])
        r = _row_periodic(acc, per, lambda a3: a3 * (coef * (1.0 + e[1][...])))
        o[0][...] = e[0][...] + r

    extras = [(x, (bm, bn), lambda i, j: (i, j)),
              (gate, (1, per, bn), lambda i, j: (i * bm // rows_per_group, 0, j))]
    (out,) = _mm_call(name, m, n, bm, bn, [a], [(w_bf, 0)], extras, [F32], body, vmem_mib=56)
    return out


def _head_rms(acc, gain, head_dim):
    outs = []
    for hh in range(acc.shape[1] // head_dim):
        y = acc[:, hh * head_dim:(hh + 1) * head_dim]
        ms = jnp.mean(y * y, axis=-1, keepdims=True)
        outs.append(y * lax.rsqrt(ms + EPS) * gain)
    return outs


def _proj_call(name, h, w_bf, col_off, n, mode, gain=None, head_dim=LANES, q_scale=1.0, bias=None):
    m, k = h.shape
    bm = _pick(m, (1024, 512, 256, 128, 64, 32, 8))
    bn = _pick(n, (256, 128))
    extras = []
    if mode in ("q", "k"):
        extras.append((gain.reshape(1, head_dim), (1, head_dim), lambda i, j: (0, 0)))
    if mode == "f":
        extras.append((bias, (1, bn), lambda i, j: (0, j)))
    out_dtypes = {"q": [BF16], "k": [F32, BF16], "v": [F32, BF16], "u": [F32], "f": [F32]}[mode]

    def body(a, w, e, o):
        acc = _dot(a[0][...], w[0][...])
        if mode in ("q", "k"):
            parts = _head_rms(acc, e[0][...], head_dim)
            for hh, y in enumerate(parts):
                sl = slice(hh * head_dim, (hh + 1) * head_dim)
                if mode == "q":
                    o[0][:, sl] = (y * q_scale).astype(BF16)
                else:
                    o[0][:, sl] = y
                    o[1][:, sl] = y.astype(BF16)
        elif mode == "v":
            o[0][...] = acc
            o[1][...] = acc.astype(BF16)
        elif mode == "u":
            o[0][...] = acc
        else:
            z = acc + e[0][...]
            o[0][...] = jnp.minimum(z, 0.0) - jnp.log1p(jnp.exp(-jnp.abs(z)))

    assert col_off % bn == 0
    return _mm_call(name, m, n, bm, bn, [h], [(w_bf, col_off // bn)], extras, out_dtypes, body, vmem_mib=48)


def _cumsum_call(lt):
    nb, nh, lp = lt.shape
    ck = CUMSUM_CHUNK
    assert lp % ck == 0

    def kern(x_ref, o_ref):
        row = lax.broadcasted_iota(jnp.int32, (ck, ck), 0)
        col = lax.broadcasted_iota(jnp.int32, (ck, ck), 1)
        upper = (row <= col).astype(F32)
        carry = jnp.zeros((nh, 1), F32)
        for c in range(lp // ck):
            xs = x_ref[0, :, c * ck:(c + 1) * ck]
            ys = jnp.dot(xs, upper, preferred_element_type=F32, precision=lax.Precision.HIGHEST) + carry
            o_ref[0, :, c * ck:(c + 1) * ck] = ys
            carry = ys[:, ck - 1:ck]

    return pl.pallas_call(
        kern,
        out_shape=jax.ShapeDtypeStruct((nb, nh, lp), F32),
        grid=(nb,),
        in_specs=[pl.BlockSpec((1, nh, lp), lambda b: (b, 0, 0))],
        out_specs=pl.BlockSpec((1, nh, lp), lambda b: (b, 0, 0)),
        compiler_params=_params(1, 32),
        name="logf_cumsum",
    )(lt)


def _attn_prompt_call(q, k, v, f_col, f_row, nb, seq, nh, head_dim):
    tq = _pick(seq, (256, 128, 64, 32, 8))
    tk = tq
    nq = seq // tq

    def kern(q_ref, k_ref, v_ref, fc_ref, fr_ref, o_ref, fq_ref):
        hh = pl.program_id(1)
        fall = fc_ref[...]
        lane = lax.broadcasted_iota(jnp.int32, fall.shape, 1)
        fq_ref[...] = jnp.sum(jnp.where(lane == hh, fall, 0.0), axis=1, keepdims=True)

        def block(qv, fq, kb, carry, diag):
            m_i, l_i, acc = carry
            ks = pl.multiple_of(kb * tk, tk)
            kv = k_ref[pl.ds(ks, tk), :]
            vv = v_ref[pl.ds(ks, tk), :]
            fk = fr_ref[0, 0, pl.ds(kb, 1), :]
            s = _dot_nt(qv, kv) + fq - fk
            if diag:
                r = lax.broadcasted_iota(jnp.int32, (tq, tk), 0)
                c = lax.broadcasted_iota(jnp.int32, (tq, tk), 1)
                s = jnp.where(c <= r, s, NEG_BIG)
            m_new = jnp.maximum(m_i, jnp.max(s, axis=1, keepdims=True))
            p = jnp.exp(s - m_new)
            alpha = jnp.exp(m_i - m_new)
            l_new = alpha * l_i + jnp.sum(p, axis=1, keepdims=True)
            acc_new = alpha * acc + _dot(p.astype(BF16), vv)
            return m_new, l_new, acc_new

        for qi in range(nq):
            qv = q_ref[qi * tq:(qi + 1) * tq, :]
            fq = fq_ref[qi * tq:(qi + 1) * tq, :]
            carry = (jnp.full((tq, 1), NEG_BIG, F32), jnp.zeros((tq, 1), F32),
                     jnp.zeros((tq, head_dim), F32))
            if qi > 0:
                carry = lax.fori_loop(0, qi, lambda kb, cr: block(qv, fq, kb, cr, False), carry)
            m_i, l_i, acc = block(qv, fq, qi, carry, True)
            o_ref[qi * tq:(qi + 1) * tq, :] = (acc / l_i).astype(BF16)

    return pl.pallas_call(
        kern,
        out_shape=jax.ShapeDtypeStruct((nb * seq, nh * head_dim), BF16),
        grid=(nb, nh),
        in_specs=[pl.BlockSpec((seq, head_dim), lambda b, h: (b, h)),
                  pl.BlockSpec((seq, head_dim), lambda b, h: (b, h)),
                  pl.BlockSpec((seq, head_dim), lambda b, h: (b, h)),
                  pl.BlockSpec((seq, nh), lambda b, h: (b, 0)),
                  pl.BlockSpec((1, 1, seq // tk, tk), lambda b, h: (b, h, 0, 0))],
        out_specs=pl.BlockSpec((seq, head_dim), lambda b, h: (b, h)),
        scratch_shapes=[pltpu.VMEM((seq, 1), F32)],
        compiler_params=_params(2, 32),
        name="attn_prompt",
    )(q, k, v, f_col, f_row)


def _attn_sample_call(q, k_new, v_new, cache_k, cache_v, fq_col, f_cache_row, f_new_row,
                      nb, s_len, past, nh, head_dim):
    tk = _pick(past, (512, 256, 128))
    nkb = past // tk
    dm = nh * head_dim

    def kern(q_ref, kn_ref, vn_ref, ck_ref, cv_ref, fq_ref, fc_ref, fn_ref, o_ref, m_ref, l_ref, acc_ref):
        kb = pl.program_id(1)

        @pl.when(kb == 0)
        def _():
            m_ref[...] = jnp.full(m_ref.shape, NEG_BIG, F32)
            l_ref[...] = jnp.zeros(l_ref.shape, F32)
            acc_ref[...] = jnp.zeros(acc_ref.shape, F32)

        def update(hh, s, vv):
            sl = slice(hh * head_dim, (hh + 1) * head_dim)
            m_i = m_ref[hh]
            m_new = jnp.maximum(m_i, jnp.max(s, axis=1, keepdims=True))
            p = jnp.exp(s - m_new)
            alpha = jnp.exp(m_i - m_new)
            l_ref[hh] = alpha * l_ref[hh] + jnp.sum(p, axis=1, keepdims=True)
            acc_ref[:, sl] = alpha * acc_ref[:, sl] + _dot(p.astype(BF16), vv)
            m_ref[hh] = m_new

        fq_all = fq_ref[0]
        for hh in range(nh):
            sl = slice(hh * head_dim, (hh + 1) * head_dim)
            kv = ck_ref[0, :, sl].astype(BF16)
            vv = cv_ref[0, :, sl].astype(BF16)
            s = _dot_nt(q_ref[:, sl], kv) + fq_all[:, hh:hh + 1] - fc_ref[0, 0, hh:hh + 1, :]
            update(hh, s, vv)

        @pl.when(kb == nkb - 1)
        def _():
            r = lax.broadcasted_iota(jnp.int32, (s_len, s_len), 0)
            c = lax.broadcasted_iota(jnp.int32, (s_len, s_len), 1)
            for hh in range(nh):
                sl = slice(hh * head_dim, (hh + 1) * head_dim)
                s = _dot_nt(q_ref[:, sl], kn_ref[:, sl]) + fq_all[:, hh:hh + 1] - fn_ref[0, hh:hh + 1, :]
                s = jnp.where(c <= r, s, NEG_BIG)
                update(hh, s, vn_ref[:, sl])
                o_ref[:, sl] = (acc_ref[:, sl] / l_ref[hh]).astype(BF16)

    return pl.pallas_call(
        kern,
        out_shape=jax.ShapeDtypeStruct((nb * s_len, dm), BF16),
        grid=(nb, nkb),
        in_specs=[pl.BlockSpec((s_len, dm), lambda b, j: (b, 0)),
                  pl.BlockSpec((s_len, dm), lambda b, j: (b, 0)),
                  pl.BlockSpec((s_len, dm), lambda b, j: (b, 0)),
                  pl.BlockSpec((1, tk, dm), lambda b, j: (b, j, 0)),
                  pl.BlockSpec((1, tk, dm), lambda b, j: (b, j, 0)),
                  pl.BlockSpec((1, s_len, nh), lambda b, j: (b, 0, 0)),
                  pl.BlockSpec((1, 1, nh, tk), lambda b, j: (b, j, 0, 0)),
                  pl.BlockSpec((1, nh, s_len), lambda b, j: (b, 0, 0))],
        out_specs=pl.BlockSpec((s_len, dm), lambda b, j: (b, 0)),
        scratch_shapes=[pltpu.VMEM((nh, s_len, 1), F32),
                        pltpu.VMEM((nh, s_len, 1), F32),
                        pltpu.VMEM((s_len, dm), F32)],
        compiler_params=_params(2, 48),
        name="attn_sample",
    )(q, k_new, v_new, cache_k, cache_v, fq_col, f_cache_row, f_new_row)


def _s5_call(name, u_tm, wb, wc, a_pack, d_row, h0_re, h0_im, nb, t_len):
    rows, d_ssm = u_tm.shape
    n_gt = wb.shape[0]
    cw = wb.shape[1]
    sw = wb.shape[2] // 2
    tc = _pick(t_len, (64, 32, 16, 8, 4, 2, 1))
    n_chunks = t_len // tc
    rc = tc * nb

    def kern(u_ref, wb_ref, wc_ref, a_ref, d_ref, h0r_ref, h0i_ref, y_ref, hr_out, hi_out,
             hr_ref, hi_ref, bu_ref):
        c = pl.program_id(1)

        @pl.when(c == 0)
        def _():
            hr_ref[...] = h0r_ref[...]
            hi_ref[...] = h0i_ref[...]

        uv = u_ref[...]
        bu_ref[...] = _dot(uv.astype(BF16), wb_ref[0])
        ar = jnp.broadcast_to(a_ref[0, 0:1, :], (nb, sw))
        ai = jnp.broadcast_to(a_ref[0, 1:2, :], (nb, sw))

        def step(t, carry):
            hr, hi = carry
            r0 = pl.multiple_of(t * nb, nb)
            bur = bu_ref[pl.ds(r0, nb), 0:sw]
            bui = bu_ref[pl.ds(r0, nb), sw:2 * sw]
            nhr = ar * hr - ai * hi + bur
            nhi = ar * hi + ai * hr + bui
            bu_ref[pl.ds(r0, nb), 0:sw] = nhr
            bu_ref[pl.ds(r0, nb), sw:2 * sw] = nhi
            return nhr, nhi

        hr, hi = lax.fori_loop(0, tc, step, (hr_ref[...], hi_ref[...]))
        hr_ref[...] = hr
        hi_ref[...] = hi

        y = _dot(bu_ref[...].astype(BF16), wc_ref[0]) + d_ref[...] * uv
        y_ref[...] = jax.nn.gelu(y, approximate=True).astype(BF16)

        @pl.when(c == n_chunks - 1)
        def _():
            hr_out[...] = hr
            hi_out[...] = hi

    return pl.pallas_call(
        kern,
        out_shape=[jax.ShapeDtypeStruct((rows, d_ssm), BF16),
                   jax.ShapeDtypeStruct((nb, n_gt * sw), F32),
                   jax.ShapeDtypeStruct((nb, n_gt * sw), F32)],
        grid=(n_gt, n_chunks),
        in_specs=[pl.BlockSpec((rc, cw), lambda g, c: (c, g)),
                  pl.BlockSpec((1, cw, 2 * sw), lambda g, c: (g, 0, 0)),
                  pl.BlockSpec((1, 2 * sw, cw), lambda g, c: (g, 0, 0)),
                  pl.BlockSpec((1, 2, sw), lambda g, c: (g, 0, 0)),
                  pl.BlockSpec((1, cw), lambda g, c: (0, g)),
                  pl.BlockSpec((nb, sw), lambda g, c: (0, g)),
                  pl.BlockSpec((nb, sw), lambda g, c: (0, g))],
        out_specs=[pl.BlockSpec((rc, cw), lambda g, c: (c, g)),
                   pl.BlockSpec((nb, sw), lambda g, c: (0, g)),
                   pl.BlockSpec((nb, sw), lambda g, c: (0, g))],
        scratch_shapes=[pltpu.VMEM((nb, sw), F32),
                        pltpu.VMEM((nb, sw), F32),
                        pltpu.VMEM((rc, 2 * sw), F32)],
        compiler_params=_params(2, 40),
        name=name,
    )(u_tm, wb, wc, a_pack, d_row, h0_re, h0_im)


def _s5_discretize(a_re, a_im, log_dt, b_re, b_im, c_re, c_im):
    g, p = a_re.shape
    ch = b_re.shape[2]
    gpt = S5_GROUPS_PER_TILE
    n_gt = g // gpt
    dt = jnp.exp(log_dt.astype(F32))[:, None]
    mag = jnp.exp(a_re * dt)
    abar_re = mag * jnp.cos(a_im * dt)
    abar_im = mag * jnp.sin(a_im * dt)
    xr = abar_re - 1
    den = a_re * a_re + a_im * a_im
    coef_re = ((xr * a_re + abar_im * a_im) / den)[..., None]
    coef_im = ((abar_im * a_re - xr * a_im) / den)[..., None]
    bb_re = coef_re * b_re - coef_im * b_im
    bb_im = coef_re * b_im + coef_im * b_re
    eye = jnp.eye(gpt, dtype=F32)

    def blockdiag_in(bb):
        t = bb.reshape(n_gt, gpt, p, ch)
        return jnp.einsum("tgpc,gh->tgchp", t, eye).reshape(n_gt, gpt * ch, gpt * p)

    def blockdiag_out(cc):
        t = cc.reshape(n_gt, gpt, ch, p)
        return jnp.einsum("tgcp,gh->tgphc", t, eye).reshape(n_gt, gpt * p, gpt * ch)

    wb = jnp.concatenate([blockdiag_in(bb_re), blockdiag_in(bb_im)], axis=2).astype(BF16)
    wc = jnp.concatenate([blockdiag_out(c_re), -blockdiag_out(c_im)], axis=1).astype(BF16)
    a_pack = jnp.stack([abar_re.reshape(n_gt, gpt * p), abar_im.reshape(n_gt, gpt * p)], axis=1)
    return wb, wc, a_pack


def _merge_call(h, o_attn, y_ssm, w_gate_bf, b_gate, w_attn_out_bf, w_glu_bf):
    m, d = h.shape
    bm = _pick(m, (512, 256, 128, 64, 32, 8))
    bn = _pick(d, (256, 128))
    nblk = d // bn

    def body(a, w, e, o):
        hv = a[0][...]
        ga = jax.nn.sigmoid(_dot(hv, w[0][...]) + e[0][...])
        gs = jax.nn.sigmoid(_dot(hv, w[1][...]) + e[1][...])
        ab = _dot(a[1][...], w[2][...])
        yv = a[2][...]
        sb = _dot(yv, w[3][...]) * jax.nn.sigmoid(_dot(yv, w[4][...]))
        o[0][...] = (ga * ab + gs * sb).astype(BF16)

    bg = b_gate.reshape(1, 2 * d)
    extras = [(bg, (1, bn), lambda i, j: (0, j)),
              (bg, (1, bn), lambda i, j: (0, j + nblk))]
    rhs = [(w_gate_bf, 0), (w_gate_bf, nblk), (w_attn_out_bf, 0), (w_glu_bf, 0), (w_glu_bf, nblk)]
    (merged,) = _mm_call("gated_merge", m, d, bm, bn, [h, o_attn, y_ssm], rhs, extras, [BF16], body,
                         vmem_mib=56)
    return merged


def _pad_time(x, lp):
    return jnp.pad(x, [(0, 0)] * (x.ndim - 1) + [(0, lp - x.shape[-1])])


def _layer(x, mod, wts, dims, cache=None, ssm_h0=None):
    nb, t_len, d = x.shape
    nh, head_dim, n_groups, n_state = dims
    d_attn = nh * head_dim
    sample = cache is not None
    m = nb * t_len
    if sample:
        xt = x.transpose(1, 0, 2).reshape(m, d)
        per, rpg = nb, m
        modg = [mod[:, j][None] for j in range(N_MOD)]
    else:
        xt = x.reshape(m, d)
        per, rpg = SUBLANES, t_len
        modg = [jnp.broadcast_to(mod[:, j][:, None, :], (nb, per, d)) for j in range(N_MOD)]

    h = _norm_mod_call(xt, wts["norm_g"][0], modg[1], modg[0], per, rpg)
    act = _ffn_in_call(h, wts["ffn1_in"])
    xt = _mm_res_call("ffn1_out", act, wts["ffn1_out"], xt, modg[2], HALF_STEP, per, rpg)

    h = _norm_mod_call(xt, wts["norm_g"][1], modg[4], modg[3], per, rpg)
    w_qkv = wts["w_qkv"]
    (q,) = _proj_call("proj_q", h, w_qkv, 0, d_attn, "q", gain=wts["q_norm_g"], head_dim=head_dim,
                      q_scale=head_dim ** -0.5)
    k32, kbf = _proj_call("proj_k", h, w_qkv, d_attn, d_attn, "k", gain=wts["k_norm_g"], head_dim=head_dim)
    v32, vbf = _proj_call("proj_v", h, w_qkv, 2 * d_attn, d_attn, "v")
    (u,) = _proj_call("proj_u", h, wts["w_u"], 0, wts["w_u"].shape[1], "u")
    (lf,) = _proj_call("proj_f", h, wts["w_f"], 0, LANES, "f", bias=wts["b_f"])
    logf = lf[:, :nh]

    ck = CUMSUM_CHUNK
    if sample:
        cache_k, cache_v, cache_logf = cache
        past = cache_k.shape[1]
        logf_bt = logf.reshape(t_len, nb, nh).transpose(1, 0, 2)
        lt = jnp.concatenate([cache_logf.astype(F32), logf_bt], axis=1).transpose(0, 2, 1)
        lp = -(-(past + t_len) // ck) * ck
        f_row = _cumsum_call(_pad_time(lt, lp))
        tk = _pick(past, (512, 256, 128))
        f_cache_row = f_row[:, :, :past].reshape(nb, nh, past // tk, tk).transpose(0, 2, 1, 3)
        f_new_row = f_row[:, :, past:past + t_len]
        fq_col = f_new_row.transpose(0, 2, 1)
        to_bt = lambda a: a.reshape(t_len, nb, -1).transpose(1, 0, 2).reshape(m, -1)
        o_bt = _attn_sample_call(to_bt(q), to_bt(kbf), to_bt(vbf),
                                 cache_k.reshape(nb, past, d_attn), cache_v.reshape(nb, past, d_attn),
                                 fq_col, f_cache_row, f_new_row, nb, t_len, past, nh, head_dim)
        o_attn = o_bt.reshape(nb, t_len, d_attn).transpose(1, 0, 2).reshape(m, d_attn)
        k_out = to_bt(k32).reshape(nb, t_len, nh, head_dim)
        v_out = to_bt(v32).reshape(nb, t_len, nh, head_dim)
        logf_out = logf_bt
        u_tm = u
        h0_re = ssm_h0[0].astype(F32).reshape(nb, n_groups * n_state)
        h0_im = ssm_h0[1].astype(F32).reshape(nb, n_groups * n_state)
    else:
        lt = logf.reshape(nb, t_len, nh).transpose(0, 2, 1)
        lp = -(-t_len // ck) * ck
        f_row = _cumsum_call(_pad_time(lt, lp))[:, :, :t_len]
        tq = _pick(t_len, (256, 128, 64, 32, 8))
        f_col = f_row.transpose(0, 2, 1).reshape(m, nh)
        o_attn = _attn_prompt_call(q, kbf, vbf, f_col, f_row.reshape(nb, nh, t_len // tq, tq),
                                   nb, t_len, nh, head_dim)
        k_out = k32.reshape(nb, t_len, nh, head_dim)
        v_out = v32.reshape(nb, t_len, nh, head_dim)
        logf_out = logf.reshape(nb, t_len, nh)
        u_tm = u.reshape(nb, t_len, -1).transpose(1, 0, 2).reshape(m, -1)
        h0_re = jnp.zeros((nb, n_groups * n_state), F32)
        h0_im = h0_re

    y_tm, h_re, h_im = _s5_call("s5_sample" if sample else "s5_prompt", u_tm, wts["s5_wb"], wts["s5_wc"],
                                wts["s5_a"], wts["s5_d"], h0_re, h0_im, nb, t_len)
    if sample:
        y_ssm = y_tm
    else:
        y_ssm = y_tm.reshape(t_len, nb, -1).transpose(1, 0, 2).reshape(m, -1)

    merged = _merge_call(h, o_attn, y_ssm, wts["w_gate"], wts["b_gate"], wts["w_attn_out"], wts["w_glu"])
    xt = _mm_res_call("mix_out", merged, wts["w_out"], xt, modg[5], 1.0, per, rpg)

    h = _norm_mod_call(xt, wts["norm_g"][2], modg[7], modg[6], per, rpg)
    act = _ffn_in_call(h, wts["ffn2_in"])
    xt = _mm_res_call("ffn2_out", act, wts["ffn2_out"], xt, modg[8], HALF_STEP, per, rpg)

    if sample:
        y = xt.reshape(t_len, nb, d).transpose(1, 0, 2)
    else:
        y = xt.reshape(nb, t_len, d)
    return (y, k_out, v_out, logf_out,
            h_re.reshape(nb, n_groups, n_state), h_im.reshape(nb, n_groups, n_state))


def kernel(x_prompt, x_sample, cache_k, cache_v, cache_logf, state_ssm_re, state_ssm_im, c_prompt, c_sample, w_ada, b_ada, norm_g, w_ffn1_in, w_ffn1_out, w_in, b_forget, q_norm_g, k_norm_g, w_attn_out, ssm_a_re, ssm_a_im, ssm_log_dt, ssm_b_re, ssm_b_im, ssm_c_re, ssm_c_im, ssm_d, w_glu, w_gate, b_gate, w_out, w_ffn2_in, w_ffn2_out):
    depth = w_ada.shape[0]
    nbp = x_prompt.shape[0]
    d = x_prompt.shape[2]
    nh, head_dim = cache_k.shape[3], cache_k.shape[4]
    n_groups, n_state, n_ch = ssm_b_re.shape[1:]
    d_attn = nh * head_dim
    d_ssm = n_groups * n_ch
    dims = (nh, head_dim, n_groups, n_state)
    assert head_dim == LANES and nh <= LANES
    assert S5_GROUPS_PER_TILE * n_ch == LANES and n_groups % S5_GROUPS_PER_TILE == 0

    xp, xs = x_prompt, x_sample
    outs_p, outs_s = [], []
    for l in range(depth):
        mod = _ada_call(jnp.concatenate([c_prompt, c_sample], axis=0), w_ada[l], b_ada[l])
        mod = mod.reshape(mod.shape[0], N_MOD, d)
        wb, wc, a_pack = _s5_discretize(ssm_a_re[l].astype(F32), ssm_a_im[l].astype(F32), ssm_log_dt[l],
                                        ssm_b_re[l].astype(F32), ssm_b_im[l].astype(F32),
                                        ssm_c_re[l].astype(F32), ssm_c_im[l].astype(F32))
        w_f = jnp.pad(w_in[l][:, 3 * d_attn:3 * d_attn + nh], ((0, 0), (0, LANES - nh))).astype(BF16)
        b_f = jnp.pad(b_forget[l].astype(F32), (0, LANES - nh)).reshape(1, LANES)
        wts = {
            "norm_g": norm_g[l].astype(F32),
            "ffn1_in": w_ffn1_in[l].astype(BF16), "ffn1_out": w_ffn1_out[l].astype(BF16),
            "ffn2_in": w_ffn2_in[l].astype(BF16), "ffn2_out": w_ffn2_out[l].astype(BF16),
            "w_qkv": w_in[l][:, :3 * d_attn].astype(BF16),
            "w_f": w_f, "b_f": b_f,
            "w_u": w_in[l][:, 3 * d_attn + nh:].astype(BF16),
            "q_norm_g": q_norm_g[l].astype(F32), "k_norm_g": k_norm_g[l].astype(F32),
            "w_attn_out": w_attn_out[l].astype(BF16), "w_glu": w_glu[l].astype(BF16),
            "w_gate": w_gate[l].astype(BF16), "b_gate": b_gate[l].astype(F32),
            "w_out": w_out[l].astype(BF16),
            "s5_wb": wb, "s5_wc": wc, "s5_a": a_pack, "s5_d": ssm_d[l].astype(F32).reshape(1, d_ssm),
        }
        xp, k1, v1, f1, r1, i1 = _layer(xp, mod[:nbp], wts, dims)
        xs, k2, v2, f2, r2, i2 = _layer(xs, mod[nbp:], wts, dims,
                                        cache=(cache_k[l], cache_v[l], cache_logf[l]),
                                        ssm_h0=(state_ssm_re[l], state_ssm_im[l]))
        outs_p.append((k1, v1, f1, r1, i1))
        outs_s.append((k2, v2, f2, r2, i2))

    stack = lambda outs, idx: jnp.stack([o[idx] for o in outs])
    return (xp, xs,
            stack(outs_p, 0), stack(outs_p, 1), stack(outs_p, 2), stack(outs_p, 3), stack(outs_p, 4),
            stack(outs_s, 0), stack(outs_s, 1), stack(outs_s, 2), stack(outs_s, 3), stack(outs_s, 4))
```

```python
import functools
import math

import jax
import jax.numpy as jnp
from jax import lax
from jax.experimental import pallas as pl
from jax.experimental.pallas import tpu as pltpu

F32 = jnp.float32
BF16 = jnp.bfloat16

EPS = 1e-6
HALF_STEP = 0.5
N_MOD = 9
NEG_BIG = -1e30
LOG2E = math.log2(math.e)
LANES = 128
SUBLANES = 8
MIB = 1024 * 1024
CUMSUM_CHUNK = 256
S5_GROUPS_PER_TILE = 8


def _params(n_axes, vmem_mib):
    return pltpu.CompilerParams(
        dimension_semantics=("arbitrary",) * n_axes,
        vmem_limit_bytes=vmem_mib * MIB,
    )


def _dot(a, b):
    return jnp.dot(a, b, preferred_element_type=F32)


def _dot_nt(a, b):
    return lax.dot_general(a, b, (((1,), (1,)), ((), ())), preferred_element_type=F32)


def _row_periodic(acc, per, fn):
    m, n = acc.shape
    return fn(acc.reshape(m // per, per, n)).reshape(m, n)


def _mm_call(name, m, n, bm, bn, lhs, rhs, extras, out_dtypes, body, vmem_mib=48):
    assert m % bm == 0 and n % bn == 0, (name, m, n, bm, bn)
    in_specs = []
    for a in lhs:
        in_specs.append(pl.BlockSpec((bm, a.shape[1]), lambda i, j: (i, 0)))
    for w, off in rhs:
        in_specs.append(pl.BlockSpec((w.shape[0], bn), lambda i, j, off=off: (0, j + off)))
    for _, bs, im in extras:
        in_specs.append(pl.BlockSpec(bs, im))
    out_specs = [pl.BlockSpec((bm, bn), lambda i, j: (i, j)) for _ in out_dtypes]
    out_shape = [jax.ShapeDtypeStruct((m, n), dt) for dt in out_dtypes]
    nl, nr, ne = len(lhs), len(rhs), len(extras)

    def kern(*refs):
        body(refs[:nl], refs[nl:nl + nr], refs[nl + nr:nl + nr + ne], refs[nl + nr + ne:])

    outs = pl.pallas_call(
        kern,
        out_shape=out_shape,
        grid=(m // bm, n // bn),
        in_specs=in_specs,
        out_specs=out_specs,
        compiler_params=_params(2, vmem_mib),
        name=name,
    )(*lhs, *[w for w, _ in rhs], *[e for e, _, _ in extras])
    return outs


def _pick(total, prefs):
    for p in prefs:
        if total % p == 0:
            return p
    return total


def _ada_call(c, w_ada, b_ada):
    mb, d = c.shape
    n = w_ada.shape[1]
    bn = _pick(n, (512, 256, 128))

    def kern(c_ref, w_ref, b_ref, o_ref):
        o_ref[...] = _dot(c_ref[...].astype(BF16), w_ref[...].astype(BF16)) + b_ref[...]

    return pl.pallas_call(
        kern,
        out_shape=jax.ShapeDtypeStruct((mb, n), F32),
        grid=(n // bn,),
        in_specs=[pl.BlockSpec((mb, d), lambda j: (0, 0)),
                  pl.BlockSpec((d, bn), lambda j: (0, j)),
                  pl.BlockSpec((1, bn), lambda j: (0, j))],
        out_specs=pl.BlockSpec((mb, bn), lambda j: (0, j)),
        compiler_params=_params(1, 40),
        name="ada_mod",
    )(c, w_ada, b_ada.reshape(1, n))


def _norm_mod_call(x, g, scale, shift, per, rows_per_group):
    m, d = x.shape
    bm = _pick(m, (256, 128, 64, 32, 8))
    bm = min(bm, rows_per_group)

    def kern(x_ref, g_ref, sc_ref, sh_ref, o_ref):
        xv = x_ref[...]
        ms = jnp.mean(xv * xv, axis=-1, keepdims=True)
        y = xv * lax.rsqrt(ms + EPS) * g_ref[...]
        y = _row_periodic(y, per, lambda y3: y3 * (1.0 + sc_ref[...]) + sh_ref[...])
        o_ref[...] = y.astype(BF16)

    grp = lambda i: (i * bm // rows_per_group, 0, 0)
    return pl.pallas_call(
        kern,
        out_shape=jax.ShapeDtypeStruct((m, d), BF16),
        grid=(m // bm,),
        in_specs=[pl.BlockSpec((bm, d), lambda i: (i, 0)),
                  pl.BlockSpec((1, d), lambda i: (0, 0)),
                  pl.BlockSpec((1, per, d), grp),
                  pl.BlockSpec((1, per, d), grp)],
        out_specs=pl.BlockSpec((bm, d), lambda i: (i, 0)),
        compiler_params=_params(1, 40),
        name="norm_mod",
    )(x, g.reshape(1, d), scale, shift)


def _ffn_in_call(h, w_in_bf):
    m, d = h.shape
    f = w_in_bf.shape[1] // 2
    bm = _pick(m, (2048, 1024, 512, 256, 128, 64, 32, 8))
    bn = _pick(f, (256, 128))

    def body(a, w, e, o):
        hv = a[0][...]
        gate = _dot(hv, w[0][...])
        up = _dot(hv, w[1][...])
        o[0][...] = (gate * jax.nn.sigmoid(gate) * up).astype(BF16)

    (act,) = _mm_call("ffn_in", m, f, bm, bn, [h], [(w_in_bf, 0), (w_in_bf, f // bn)], [],
                      [BF16], body, vmem_mib=56)
    return act


def _mm_res_call(name, a, w_bf, x, gate, coef, per, rows_per_group):
    m, k = a.shape
    n = w_bf.shape[1]
    bm = _pick(m, (1024, 512, 256, 128, 64, 32, 8) if k <= 4096 else (512, 256, 128, 64, 32, 8))
    bm = min(bm, rows_per_group)
    bn = _pick(n, (512, 256, 128))

    def body(lhs, rhs, e, o):
        acc = _dot(lhs[0][...], rhs[0][...])
        r = _row_periodic(acc, per, lambda a3: a3 * (coef * (1.0 + e[1][...])))
        o[0][...] = e[0][...] + r

    extras = [(x, (bm, bn), lambda i, j: (i, j)),
              (gate, (1, per, bn), lambda i, j: (i * bm // rows_per_group, 0, j))]
    (out,) = _mm_call(name, m, n, bm, bn, [a], [(w_bf, 0)], extras, [F32], body, vmem_mib=56)
    return out


def _head_rms(acc, gain, head_dim):
    outs = []
    for hh in range(acc.shape[1] // head_dim):
        y = acc[:, hh * head_dim:(hh + 1) * head_dim]
        ms = jnp.mean(y * y, axis=-1, keepdims=True)
        outs.append(y * lax.rsqrt(ms + EPS) * gain)
    return outs


def _proj_call(name, h, w_bf, col_off, n, mode, gain=None, head_dim=LANES, q_scale=1.0, bias=None):
    m, k = h.shape
    bm = _pick(m, (1024, 512, 256, 128, 64, 32, 8))
    bn = _pick(n, (512, 256, 128))
    extras = []
    if mode in ("q", "k"):
        extras.append((gain.reshape(1, head_dim), (1, head_dim), lambda i, j: (0, 0)))
    if mode == "f":
        extras.append((bias, (1, bn), lambda i, j: (0, j)))
    out_dtypes = {"q": [BF16], "k": [F32, BF16], "v": [F32, BF16], "u": [F32], "f": [F32]}[mode]

    def body(a, w, e, o):
        acc = _dot(a[0][...], w[0][...])
        if mode in ("q", "k"):
            parts = _head_rms(acc, e[0][...], head_dim)
            for hh, y in enumerate(parts):
                sl = slice(hh * head_dim, (hh + 1) * head_dim)
                if mode == "q":
                    o[0][:, sl] = (y * q_scale).astype(BF16)
                else:
                    o[0][:, sl] = y
                    o[1][:, sl] = y.astype(BF16)
        elif mode == "v":
            o[0][...] = acc
            o[1][...] = acc.astype(BF16)
        elif mode == "u":
            o[0][...] = acc
        else:
            z = acc + e[0][...]
            o[0][...] = jnp.minimum(z, 0.0) - jnp.log1p(jnp.exp(-jnp.abs(z)))

    assert col_off % bn == 0
    return _mm_call(name, m, n, bm, bn, [h], [(w_bf, col_off // bn)], extras, out_dtypes, body, vmem_mib=48)


def _cumsum_call(lt):
    nb, nh, lp = lt.shape
    ck = CUMSUM_CHUNK
    assert lp % ck == 0

    def kern(x_ref, o_ref):
        row = lax.broadcasted_iota(jnp.int32, (ck, ck), 0)
        col = lax.broadcasted_iota(jnp.int32, (ck, ck), 1)
        upper = (row <= col).astype(F32)
        carry = jnp.zeros((nh, 1), F32)
        for c in range(lp // ck):
            xs = x_ref[0, :, c * ck:(c + 1) * ck]
            ys = jnp.dot(xs, upper, preferred_element_type=F32, precision=lax.Precision.HIGHEST) + carry
            o_ref[0, :, c * ck:(c + 1) * ck] = ys
            carry = ys[:, ck - 1:ck]

    return pl.pallas_call(
        kern,
        out_shape=jax.ShapeDtypeStruct((nb, nh, lp), F32),
        grid=(nb,),
        in_specs=[pl.BlockSpec((1, nh, lp), lambda b: (b, 0, 0))],
        out_specs=pl.BlockSpec((1, nh, lp), lambda b: (b, 0, 0)),
        compiler_params=_params(1, 32),
        name="logf_cumsum",
    )(lt)


def _softmax_pv(parts, fq):
    m_keys = None
    for s, _ in parts:
        mx = jnp.max(s, axis=1, keepdims=True)
        m_keys = mx if m_keys is None else jnp.maximum(m_keys, mx)
    shift = fq - (m_keys + fq)
    l_sum, acc = None, None
    for s, v in parts:
        p = jnp.exp2(s + shift)
        ls = jnp.sum(p, axis=1, keepdims=True)
        pv = _dot(p.astype(BF16), v)
        l_sum = ls if l_sum is None else l_sum + ls
        acc = pv if acc is None else acc + pv
    return acc / l_sum


def _causal_mask(s):
    r = lax.broadcasted_iota(jnp.int32, s.shape, 0)
    c = lax.broadcasted_iota(jnp.int32, s.shape, 1)
    return jnp.where(c <= r, s, NEG_BIG)


def _attn_prompt_call(q, k, v, f_col, f_row, nb, seq, nh, head_dim):
    tq = _pick(seq, (256, 128, 64, 32, 8))
    nq = seq // tq

    def kern(q_ref, k_ref, v_ref, fc_ref, fr_ref, o_ref, fq_ref):
        hh = pl.program_id(1)
        fall = fc_ref[...]
        lane = lax.broadcasted_iota(jnp.int32, fall.shape, 1)
        fq_ref[...] = jnp.sum(jnp.where(lane == hh, fall, 0.0), axis=1, keepdims=True) * LOG2E
        for qi in range(nq):
            lo, hi = qi * tq, (qi + 1) * tq
            qv = q_ref[lo:hi, :]
            s_d = _causal_mask(_dot_nt(qv, k_ref[lo:hi, :]) - fr_ref[0, 0, :, lo:hi] * LOG2E)
            parts = [(s_d, v_ref[lo:hi, :])]
            if qi > 0:
                s_o = _dot_nt(qv, k_ref[0:lo, :]) - fr_ref[0, 0, :, 0:lo] * LOG2E
                parts.append((s_o, v_ref[0:lo, :]))
            o_ref[lo:hi, :] = _softmax_pv(parts, fq_ref[lo:hi, :]).astype(BF16)

    return pl.pallas_call(
        kern,
        out_shape=jax.ShapeDtypeStruct((nb * seq, nh * head_dim), BF16),
        grid=(nb, nh),
        in_specs=[pl.BlockSpec((seq, head_dim), lambda b, h: (b, h)),
                  pl.BlockSpec((seq, head_dim), lambda b, h: (b, h)),
                  pl.BlockSpec((seq, head_dim), lambda b, h: (b, h)),
                  pl.BlockSpec((seq, nh), lambda b, h: (b, 0)),
                  pl.BlockSpec((1, 1, 1, seq), lambda b, h: (b, h, 0, 0))],
        out_specs=pl.BlockSpec((seq, head_dim), lambda b, h: (b, h)),
        scratch_shapes=[pltpu.VMEM((seq, 1), F32)],
        compiler_params=_params(2, 40),
        name="attn_prompt",
    )(q, k, v, f_col, f_row)


def _attn_sample_call(q, k_new, v_new, cache_k, cache_v, fq_col, f_cache_row, f_new_row,
                      nb, s_len, past, nh, head_dim):
    n_steps = nb * nh

    def kern(q_ref, kn_ref, vn_ref, fq_ref, fc_ref, fn_ref, ck_hbm, cv_hbm, o_ref, kbuf, vbuf, sem):
        step = pl.program_id(0) * nh + pl.program_id(1)
        slot = lax.rem(step, 2)

        def copies(s_idx, slot_idx):
            bb = s_idx // nh
            hh = lax.rem(s_idx, nh)
            return (pltpu.make_async_copy(ck_hbm.at[bb, :, hh, :], kbuf.at[slot_idx], sem.at[0, slot_idx]),
                    pltpu.make_async_copy(cv_hbm.at[bb, :, hh, :], vbuf.at[slot_idx], sem.at[1, slot_idx]))

        @pl.when(step == 0)
        def _():
            for cp in copies(step, slot):
                cp.start()

        @pl.when(step + 1 < n_steps)
        def _():
            for cp in copies(step + 1, 1 - slot):
                cp.start()

        for cp in copies(step, slot):
            cp.wait()

        qv = q_ref[...]
        s_c = _dot_nt(qv, kbuf[slot].astype(BF16)) - fc_ref[0, 0] * LOG2E
        s_n = _causal_mask(_dot_nt(qv, kn_ref[...]) - fn_ref[0, 0] * LOG2E)
        parts = [(s_c, vbuf[slot].astype(BF16)), (s_n, vn_ref[...])]
        o_ref[...] = _softmax_pv(parts, fq_ref[0, 0] * LOG2E).astype(BF16)

    return pl.pallas_call(
        kern,
        out_shape=jax.ShapeDtypeStruct((nb * s_len, nh * head_dim), BF16),
        grid=(nb, nh),
        in_specs=[pl.BlockSpec((s_len, head_dim), lambda b, h: (b, h)),
                  pl.BlockSpec((s_len, head_dim), lambda b, h: (b, h)),
                  pl.BlockSpec((s_len, head_dim), lambda b, h: (b, h)),
                  pl.BlockSpec((1, 1, s_len, 1), lambda b, h: (b, h, 0, 0)),
                  pl.BlockSpec((1, 1, 1, past), lambda b, h: (b, h, 0, 0)),
                  pl.BlockSpec((1, 1, 1, s_len), lambda b, h: (b, h, 0, 0)),
                  pl.BlockSpec(memory_space=pl.ANY),
                  pl.BlockSpec(memory_space=pl.ANY)],
        out_specs=pl.BlockSpec((s_len, head_dim), lambda b, h: (b, h)),
        scratch_shapes=[pltpu.VMEM((2, past, head_dim), F32),
                        pltpu.VMEM((2, past, head_dim), F32),
                        pltpu.SemaphoreType.DMA((2, 2))],
        compiler_params=_params(2, 40),
        name="attn_sample",
    )(q, k_new, v_new, fq_col, f_cache_row, f_new_row, cache_k, cache_v)


def _s5_call(name, u_tm, wb, wc, a_pack, d_row, h0_re, h0_im, nb, t_len):
    rows, d_ssm = u_tm.shape
    n_gt = wb.shape[0]
    cw = wb.shape[1]
    sw = wb.shape[2] // 2
    tc = _pick(t_len, (64, 32, 16, 8, 4, 2, 1))
    n_chunks = t_len // tc
    rc = tc * nb

    def kern(u_ref, wb_ref, wc_ref, a_ref, d_ref, h0r_ref, h0i_ref, y_ref, hr_out, hi_out,
             hr_ref, hi_ref, bu_ref):
        c = pl.program_id(1)

        @pl.when(c == 0)
        def _():
            hr_ref[...] = h0r_ref[...]
            hi_ref[...] = h0i_ref[...]

        uv = u_ref[...]
        bu_ref[...] = _dot(uv.astype(BF16), wb_ref[0])
        ar = jnp.broadcast_to(a_ref[0, 0:1, :], (nb, sw))
        ai = jnp.broadcast_to(a_ref[0, 1:2, :], (nb, sw))

        def step(t, carry):
            hr, hi = carry
            r0 = pl.multiple_of(t * nb, nb)
            bur = bu_ref[pl.ds(r0, nb), 0:sw]
            bui = bu_ref[pl.ds(r0, nb), sw:2 * sw]
            nhr = ar * hr - ai * hi + bur
            nhi = ar * hi + ai * hr + bui
            bu_ref[pl.ds(r0, nb), 0:sw] = nhr
            bu_ref[pl.ds(r0, nb), sw:2 * sw] = nhi
            return nhr, nhi

        hr, hi = lax.fori_loop(0, tc, step, (hr_ref[...], hi_ref[...]), unroll=min(tc, 8))
        hr_ref[...] = hr
        hi_ref[...] = hi

        y = _dot(bu_ref[...].astype(BF16), wc_ref[0]) + d_ref[...] * uv
        y_ref[...] = jax.nn.gelu(y, approximate=True).astype(BF16)

        @pl.when(c == n_chunks - 1)
        def _():
            hr_out[...] = hr
            hi_out[...] = hi

    return pl.pallas_call(
        kern,
        out_shape=[jax.ShapeDtypeStruct((rows, d_ssm), BF16),
                   jax.ShapeDtypeStruct((nb, n_gt * sw), F32),
                   jax.ShapeDtypeStruct((nb, n_gt * sw), F32)],
        grid=(n_gt, n_chunks),
        in_specs=[pl.BlockSpec((rc, cw), lambda g, c: (c, g)),
                  pl.BlockSpec((1, cw, 2 * sw), lambda g, c: (g, 0, 0)),
                  pl.BlockSpec((1, 2 * sw, cw), lambda g, c: (g, 0, 0)),
                  pl.BlockSpec((1, 2, sw), lambda g, c: (g, 0, 0)),
                  pl.BlockSpec((1, cw), lambda g, c: (0, g)),
                  pl.BlockSpec((nb, sw), lambda g, c: (0, g)),
                  pl.BlockSpec((nb, sw), lambda g, c: (0, g))],
        out_specs=[pl.BlockSpec((rc, cw), lambda g, c: (c, g)),
                   pl.BlockSpec((nb, sw), lambda g, c: (0, g)),
                   pl.BlockSpec((nb, sw), lambda g, c: (0, g))],
        scratch_shapes=[pltpu.VMEM((nb, sw), F32),
                        pltpu.VMEM((nb, sw), F32),
                        pltpu.VMEM((rc, 2 * sw), F32)],
        compiler_params=_params(2, 40),
        name=name,
    )(u_tm, wb, wc, a_pack, d_row, h0_re, h0_im)


def _s5_discretize(a_re, a_im, log_dt, b_re, b_im, c_re, c_im):
    g, p = a_re.shape
    ch = b_re.shape[2]
    gpt = S5_GROUPS_PER_TILE
    n_gt = g // gpt
    dt = jnp.exp(log_dt.astype(F32))[:, None]
    mag = jnp.exp(a_re * dt)
    abar_re = mag * jnp.cos(a_im * dt)
    abar_im = mag * jnp.sin(a_im * dt)
    xr = abar_re - 1
    den = a_re * a_re + a_im * a_im
    coef_re = ((xr * a_re + abar_im * a_im) / den)[..., None]
    coef_im = ((abar_im * a_re - xr * a_im) / den)[..., None]
    bb_re = coef_re * b_re - coef_im * b_im
    bb_im = coef_re * b_im + coef_im * b_re
    eye = jnp.eye(gpt, dtype=F32)

    def blockdiag_in(bb):
        t = bb.reshape(n_gt, gpt, p, ch)
        return jnp.einsum("tgpc,gh->tgchp", t, eye).reshape(n_gt, gpt * ch, gpt * p)

    def blockdiag_out(cc):
        t = cc.reshape(n_gt, gpt, ch, p)
        return jnp.einsum("tgcp,gh->tgphc", t, eye).reshape(n_gt, gpt * p, gpt * ch)

    wb = jnp.concatenate([blockdiag_in(bb_re), blockdiag_in(bb_im)], axis=2).astype(BF16)
    wc = jnp.concatenate([blockdiag_out(c_re), -blockdiag_out(c_im)], axis=1).astype(BF16)
    a_pack = jnp.stack([abar_re.reshape(n_gt, gpt * p), abar_im.reshape(n_gt, gpt * p)], axis=1)
    return wb, wc, a_pack


def _merge_call(h, o_attn, y_ssm, w_gate_bf, b_gate, w_attn_out_bf, w_glu_bf):
    m, d = h.shape
    bm = _pick(m, (512, 256, 128, 64, 32, 8))
    bn = _pick(d, (256, 128))
    nblk = d // bn

    def body(a, w, e, o):
        hv = a[0][...]
        ga = jax.nn.sigmoid(_dot(hv, w[0][...]) + e[0][...])
        gs = jax.nn.sigmoid(_dot(hv, w[1][...]) + e[1][...])
        ab = _dot(a[1][...], w[2][...])
        yv = a[2][...]
        sb = _dot(yv, w[3][...]) * jax.nn.sigmoid(_dot(yv, w[4][...]))
        o[0][...] = (ga * ab + gs * sb).astype(BF16)

    bg = b_gate.reshape(1, 2 * d)
    extras = [(bg, (1, bn), lambda i, j: (0, j)),
              (bg, (1, bn), lambda i, j: (0, j + nblk))]
    rhs = [(w_gate_bf, 0), (w_gate_bf, nblk), (w_attn_out_bf, 0), (w_glu_bf, 0), (w_glu_bf, nblk)]
    (merged,) = _mm_call("gated_merge", m, d, bm, bn, [h, o_attn, y_ssm], rhs, extras, [BF16], body,
                         vmem_mib=56)
    return merged


def _pad_time(x, lp):
    return jnp.pad(x, [(0, 0)] * (x.ndim - 1) + [(0, lp - x.shape[-1])])


def _layer(x, mod, wts, dims, cache=None, ssm_h0=None):
    nb, t_len, d = x.shape
    nh, head_dim, n_groups, n_state = dims
    d_attn = nh * head_dim
    sample = cache is not None
    m = nb * t_len
    if sample:
        xt = x.transpose(1, 0, 2).reshape(m, d)
        per, rpg = nb, m
        modg = [mod[:, j][None] for j in range(N_MOD)]
    else:
        xt = x.reshape(m, d)
        per, rpg = SUBLANES, t_len
        modg = [jnp.broadcast_to(mod[:, j][:, None, :], (nb, per, d)) for j in range(N_MOD)]

    h = _norm_mod_call(xt, wts["norm_g"][0], modg[1], modg[0], per, rpg)
    act = _ffn_in_call(h, wts["ffn1_in"])
    xt = _mm_res_call("ffn1_out", act, wts["ffn1_out"], xt, modg[2], HALF_STEP, per, rpg)

    h = _norm_mod_call(xt, wts["norm_g"][1], modg[4], modg[3], per, rpg)
    w_qkv = wts["w_qkv"]
    (q,) = _proj_call("proj_q", h, w_qkv, 0, d_attn, "q", gain=wts["q_norm_g"], head_dim=head_dim,
                      q_scale=head_dim ** -0.5 * LOG2E)
    k32, kbf = _proj_call("proj_k", h, w_qkv, d_attn, d_attn, "k", gain=wts["k_norm_g"], head_dim=head_dim)
    v32, vbf = _proj_call("proj_v", h, w_qkv, 2 * d_attn, d_attn, "v")
    (u,) = _proj_call("proj_u", h, wts["w_u"], 0, wts["w_u"].shape[1], "u")
    (lf,) = _proj_call("proj_f", h, wts["w_f"], 0, LANES, "f", bias=wts["b_f"])
    logf = lf[:, :nh]

    ck = CUMSUM_CHUNK
    if sample:
        cache_k, cache_v, cache_logf = cache
        past = cache_k.shape[1]
        logf_bt = logf.reshape(t_len, nb, nh).transpose(1, 0, 2)
        lt = jnp.concatenate([cache_logf.astype(F32), logf_bt], axis=1).transpose(0, 2, 1)
        lp = -(-(past + t_len) // ck) * ck
        f_row = _cumsum_call(_pad_time(lt, lp))
        f_cache_row = f_row[:, :, None, :past]
        f_new_row = f_row[:, :, None, past:past + t_len]
        fq_col = f_row[:, :, past:past + t_len, None]
        to_bt = lambda a: a.reshape(t_len, nb, -1).transpose(1, 0, 2).reshape(m, -1)
        o_bt = _attn_sample_call(to_bt(q), to_bt(kbf), to_bt(vbf), cache_k, cache_v,
                                 fq_col, f_cache_row, f_new_row, nb, t_len, past, nh, head_dim)
        o_attn = o_bt.reshape(nb, t_len, d_attn).transpose(1, 0, 2).reshape(m, d_attn)
        k_out = to_bt(k32).reshape(nb, t_len, nh, head_dim)
        v_out = to_bt(v32).reshape(nb, t_len, nh, head_dim)
        logf_out = logf_bt
        u_tm = u
        h0_re = ssm_h0[0].astype(F32).reshape(nb, n_groups * n_state)
        h0_im = ssm_h0[1].astype(F32).reshape(nb, n_groups * n_state)
    else:
        lt = logf.reshape(nb, t_len, nh).transpose(0, 2, 1)
        lp = -(-t_len // ck) * ck
        f_row = _cumsum_call(_pad_time(lt, lp))[:, :, :t_len]
        f_col = f_row.transpose(0, 2, 1).reshape(m, nh)
        o_attn = _attn_prompt_call(q, kbf, vbf, f_col, f_row[:, :, None, :], nb, t_len, nh, head_dim)
        k_out = k32.reshape(nb, t_len, nh, head_dim)
        v_out = v32.reshape(nb, t_len, nh, head_dim)
        logf_out = logf.reshape(nb, t_len, nh)
        u_tm = u.reshape(nb, t_len, -1).transpose(1, 0, 2).reshape(m, -1)
        h0_re = jnp.zeros((nb, n_groups * n_state), F32)
        h0_im = h0_re

    y_tm, h_re, h_im = _s5_call("s5_sample" if sample else "s5_prompt", u_tm, wts["s5_wb"], wts["s5_wc"],
                                wts["s5_a"], wts["s5_d"], h0_re, h0_im, nb, t_len)
    if sample:
        y_ssm = y_tm
    else:
        y_ssm = y_tm.reshape(t_len, nb, -1).transpose(1, 0, 2).reshape(m, -1)

    merged = _merge_call(h, o_attn, y_ssm, wts["w_gate"], wts["b_gate"], wts["w_attn_out"], wts["w_glu"])
    xt = _mm_res_call("mix_out", merged, wts["w_out"], xt, modg[5], 1.0, per, rpg)

    h = _norm_mod_call(xt, wts["norm_g"][2], modg[7], modg[6], per, rpg)
    act = _ffn_in_call(h, wts["ffn2_in"])
    xt = _mm_res_call("ffn2_out", act, wts["ffn2_out"], xt, modg[8], HALF_STEP, per, rpg)

    if sample:
        y = xt.reshape(t_len, nb, d).transpose(1, 0, 2)
    else:
        y = xt.reshape(nb, t_len, d)
    return (y, k_out, v_out, logf_out,
            h_re.reshape(nb, n_groups, n_state), h_im.reshape(nb, n_groups, n_state))


def kernel(x_prompt, x_sample, cache_k, cache_v, cache_logf, state_ssm_re, state_ssm_im, c_prompt, c_sample, w_ada, b_ada, norm_g, w_ffn1_in, w_ffn1_out, w_in, b_forget, q_norm_g, k_norm_g, w_attn_out, ssm_a_re, ssm_a_im, ssm_log_dt, ssm_b_re, ssm_b_im, ssm_c_re, ssm_c_im, ssm_d, w_glu, w_gate, b_gate, w_out, w_ffn2_in, w_ffn2_out):
    depth = w_ada.shape[0]
    nbp = x_prompt.shape[0]
    d = x_prompt.shape[2]
    nh, head_dim = cache_k.shape[3], cache_k.shape[4]
    n_groups, n_state, n_ch = ssm_b_re.shape[1:]
    d_attn = nh * head_dim
    d_ssm = n_groups * n_ch
    dims = (nh, head_dim, n_groups, n_state)
    assert head_dim == LANES and nh <= LANES
    assert S5_GROUPS_PER_TILE * n_ch == LANES and n_groups % S5_GROUPS_PER_TILE == 0

    xp, xs = x_prompt, x_sample
    outs_p, outs_s = [], []
    for l in range(depth):
        mod = _ada_call(jnp.concatenate([c_prompt, c_sample], axis=0), w_ada[l], b_ada[l])
        mod = mod.reshape(mod.shape[0], N_MOD, d)
        wb, wc, a_pack = _s5_discretize(ssm_a_re[l].astype(F32), ssm_a_im[l].astype(F32), ssm_log_dt[l],
                                        ssm_b_re[l].astype(F32), ssm_b_im[l].astype(F32),
                                        ssm_c_re[l].astype(F32), ssm_c_im[l].astype(F32))
        w_f = jnp.pad(w_in[l][:, 3 * d_attn:3 * d_attn + nh], ((0, 0), (0, LANES - nh))).astype(BF16)
        b_f = jnp.pad(b_forget[l].astype(F32), (0, LANES - nh)).reshape(1, LANES)
        wts = {
            "norm_g": norm_g[l].astype(F32),
            "ffn1_in": w_ffn1_in[l].astype(BF16), "ffn1_out": w_ffn1_out[l].astype(BF16),
            "ffn2_in": w_ffn2_in[l].astype(BF16), "ffn2_out": w_ffn2_out[l].astype(BF16),
            "w_qkv": w_in[l][:, :3 * d_attn].astype(BF16),
            "w_f": w_f, "b_f": b_f,
            "w_u": w_in[l][:, 3 * d_attn + nh:].astype(BF16),
            "q_norm_g": q_norm_g[l].astype(F32), "k_norm_g": k_norm_g[l].astype(F32),
            "w_attn_out": w_attn_out[l].astype(BF16), "w_glu": w_glu[l].astype(BF16),
            "w_gate": w_gate[l].astype(BF16), "b_gate": b_gate[l].astype(F32),
            "w_out": w_out[l].astype(BF16),
            "s5_wb": wb, "s5_wc": wc, "s5_a": a_pack, "s5_d": ssm_d[l].astype(F32).reshape(1, d_ssm),
        }
        xp, k1, v1, f1, r1, i1 = _layer(xp, mod[:nbp], wts, dims)
        xs, k2, v2, f2, r2, i2 = _layer(xs, mod[nbp:], wts, dims,
                                        cache=(cache_k[l], cache_v[l], cache_logf[l]),
                                        ssm_h0=(state_ssm_re[l], state_ssm_im[l]))
        outs_p.append((k1, v1, f1, r1, i1))
        outs_s.append((k2, v2, f2, r2, i2))

    stack = lambda outs, idx: jnp.stack([o[idx] for o in outs])
    return (xp, xs,
            stack(outs_p, 0), stack(outs_p, 1), stack(outs_p, 2), stack(outs_p, 3), stack(outs_p, 4),
            stack(outs_s, 0), stack(outs_s, 1), stack(outs_s, 2), stack(outs_s, 3), stack(outs_s, 4))
```

```python
import functools
import math

import jax
import jax.numpy as jnp
from jax import lax
from jax.experimental import pallas as pl
from jax.experimental.pallas import tpu as pltpu

F32 = jnp.float32
BF16 = jnp.bfloat16

EPS = 1e-6
HALF_STEP = 0.5
N_MOD = 9
NEG_BIG = -1e30
LOG2E = math.log2(math.e)
LANES = 128
SUBLANES = 8
MIB = 1024 * 1024
CUMSUM_CHUNK = 256
S5_GROUPS_PER_TILE = 8


def _params(n_axes, vmem_mib):
    return pltpu.CompilerParams(
        dimension_semantics=("arbitrary",) * n_axes,
        vmem_limit_bytes=vmem_mib * MIB,
    )


def _dot(a, b):
    return jnp.dot(a, b, preferred_element_type=F32)


def _dot_nt(a, b):
    return lax.dot_general(a, b, (((1,), (1,)), ((), ())), preferred_element_type=F32)


def _row_periodic(acc, per, fn):
    m, n = acc.shape
    return fn(acc.reshape(m // per, per, n)).reshape(m, n)


def _mm_call(name, m, n, bm, bn, lhs, rhs, extras, out_dtypes, body, vmem_mib=48, lhs_buffers=2):
    assert m % bm == 0 and n % bn == 0, (name, m, n, bm, bn)
    in_specs = []
    for a in lhs:
        mode = {} if lhs_buffers == 2 else {"pipeline_mode": pl.Buffered(lhs_buffers)}
        in_specs.append(pl.BlockSpec((bm, a.shape[1]), lambda i, j: (i, 0), **mode))
    for w, off in rhs:
        in_specs.append(pl.BlockSpec((w.shape[0], bn), lambda i, j, off=off: (0, j + off)))
    for _, bs, im in extras:
        in_specs.append(pl.BlockSpec(bs, im))
    out_specs = [pl.BlockSpec((bm, bn), lambda i, j: (i, j)) for _ in out_dtypes]
    out_shape = [jax.ShapeDtypeStruct((m, n), dt) for dt in out_dtypes]
    nl, nr, ne = len(lhs), len(rhs), len(extras)

    def kern(*refs):
        body(refs[:nl], refs[nl:nl + nr], refs[nl + nr:nl + nr + ne], refs[nl + nr + ne:])

    outs = pl.pallas_call(
        kern,
        out_shape=out_shape,
        grid=(m // bm, n // bn),
        in_specs=in_specs,
        out_specs=out_specs,
        compiler_params=_params(2, vmem_mib),
        name=name,
    )(*lhs, *[w for w, _ in rhs], *[e for e, _, _ in extras])
    return outs


def _pick(total, prefs):
    for p in prefs:
        if total % p == 0:
            return p
    return total


def _ada_call(c, w_ada, b_ada):
    mb, d = c.shape
    n = w_ada.shape[1]
    bn = _pick(n, (512, 256, 128))

    def kern(c_ref, w_ref, b_ref, o_ref):
        o_ref[...] = _dot(c_ref[...].astype(BF16), w_ref[...].astype(BF16)) + b_ref[...]

    return pl.pallas_call(
        kern,
        out_shape=jax.ShapeDtypeStruct((mb, n), F32),
        grid=(n // bn,),
        in_specs=[pl.BlockSpec((mb, d), lambda j: (0, 0)),
                  pl.BlockSpec((d, bn), lambda j: (0, j)),
                  pl.BlockSpec((1, bn), lambda j: (0, j))],
        out_specs=pl.BlockSpec((mb, bn), lambda j: (0, j)),
        compiler_params=_params(1, 40),
        name="ada_mod",
    )(c, w_ada, b_ada.reshape(1, n))


def _norm_mod_call(x, g, scale, shift, per, rows_per_group):
    m, d = x.shape
    bm = _pick(m, (256, 128, 64, 32, 8))
    bm = min(bm, rows_per_group)

    def kern(x_ref, g_ref, sc_ref, sh_ref, o_ref):
        xv = x_ref[...]
        ms = jnp.mean(xv * xv, axis=-1, keepdims=True)
        y = xv * lax.rsqrt(ms + EPS) * g_ref[...]
        y = _row_periodic(y, per, lambda y3: y3 * (1.0 + sc_ref[...]) + sh_ref[...])
        o_ref[...] = y.astype(BF16)

    grp = lambda i: (i * bm // rows_per_group, 0, 0)
    return pl.pallas_call(
        kern,
        out_shape=jax.ShapeDtypeStruct((m, d), BF16),
        grid=(m // bm,),
        in_specs=[pl.BlockSpec((bm, d), lambda i: (i, 0)),
                  pl.BlockSpec((1, d), lambda i: (0, 0)),
                  pl.BlockSpec((1, per, d), grp),
                  pl.BlockSpec((1, per, d), grp)],
        out_specs=pl.BlockSpec((bm, d), lambda i: (i, 0)),
        compiler_params=_params(1, 40),
        name="norm_mod",
    )(x, g.reshape(1, d), scale, shift)


def _ffn_in_call(h, w_in):
    m, d = h.shape
    f = w_in.shape[1] // 2
    bm = _pick(m, (2048, 1024, 512, 256, 128, 64, 32, 8))
    bn = _pick(f, (256, 128))

    def body(a, w, e, o):
        hv = a[0][...]
        gate = _dot(hv, w[0][...].astype(BF16))
        up = _dot(hv, w[1][...].astype(BF16))
        o[0][...] = (gate * jax.nn.sigmoid(gate) * up).astype(BF16)

    (act,) = _mm_call("ffn_in", m, f, bm, bn, [h], [(w_in, 0), (w_in, f // bn)], [],
                      [BF16], body, vmem_mib=56, lhs_buffers=1)
    return act


def _mm_res_call(name, a, w_bf, x, gate, coef, per, rows_per_group):
    m, k = a.shape
    n = w_bf.shape[1]
    bm = _pick(m, (1024, 512, 256, 128, 64, 32, 8) if k <= 4096 else (512, 256, 128, 64, 32, 8))
    bm = min(bm, rows_per_group)
    bn = _pick(n, (512, 256, 128))

    def body(lhs, rhs, e, o):
        acc = _dot(lhs[0][...], rhs[0][...])
        r = _row_periodic(acc, per, lambda a3: a3 * (coef * (1.0 + e[1][...])))
        o[0][...] = e[0][...] + r

    extras = [(x, (bm, bn), lambda i, j: (i, j)),
              (gate, (1, per, bn), lambda i, j: (i * bm // rows_per_group, 0, j))]
    (out,) = _mm_call(name, m, n, bm, bn, [a], [(w_bf, 0)], extras, [F32], body, vmem_mib=56)
    return out


def _head_rms(acc, gain, head_dim):
    outs = []
    for hh in range(acc.shape[1] // head_dim):
        y = acc[:, hh * head_dim:(hh + 1) * head_dim]
        ms = jnp.mean(y * y, axis=-1, keepdims=True)
        outs.append(y * lax.rsqrt(ms + EPS) * gain)
    return outs


def _proj_call(name, h, w_bf, col_off, n, mode, gain=None, head_dim=LANES, q_scale=1.0, bias=None):
    m, k = h.shape
    bm = _pick(m, (1024, 512, 256, 128, 64, 32, 8))
    bn = _pick(n, (512, 256, 128))
    extras = []
    if mode in ("q", "k"):
        extras.append((gain.reshape(1, head_dim), (1, head_dim), lambda i, j: (0, 0)))
    if mode == "f":
        extras.append((bias, (1, bn), lambda i, j: (0, j)))
    out_dtypes = {"q": [BF16], "k": [F32, BF16], "v": [F32, BF16], "u": [F32], "f": [F32]}[mode]

    def body(a, w, e, o):
        acc = _dot(a[0][...], w[0][...])
        if mode in ("q", "k"):
            parts = _head_rms(acc, e[0][...], head_dim)
            for hh, y in enumerate(parts):
                sl = slice(hh * head_dim, (hh + 1) * head_dim)
                if mode == "q":
                    o[0][:, sl] = (y * q_scale).astype(BF16)
                else:
                    o[0][:, sl] = y
                    o[1][:, sl] = y.astype(BF16)
        elif mode == "v":
            o[0][...] = acc
            o[1][...] = acc.astype(BF16)
        elif mode == "u":
            o[0][...] = acc
        else:
            z = acc + e[0][...]
            o[0][...] = jnp.minimum(z, 0.0) - jnp.log1p(jnp.exp(-jnp.abs(z)))

    assert col_off % bn == 0
    return _mm_call(name, m, n, bm, bn, [h], [(w_bf, col_off // bn)], extras, out_dtypes, body, vmem_mib=48)


def _cumsum_call(lt):
    nb, nh, lp = lt.shape
    ck = CUMSUM_CHUNK
    assert lp % ck == 0

    def kern(x_ref, o_ref):
        row = lax.broadcasted_iota(jnp.int32, (ck, ck), 0)
        col = lax.broadcasted_iota(jnp.int32, (ck, ck), 1)
        upper = (row <= col).astype(F32)
        carry = jnp.zeros((nh, 1), F32)
        for c in range(lp // ck):
            xs = x_ref[0, :, c * ck:(c + 1) * ck]
            ys = jnp.dot(xs, upper, preferred_element_type=F32, precision=lax.Precision.HIGHEST) + carry
            o_ref[0, :, c * ck:(c + 1) * ck] = ys
            carry = ys[:, ck - 1:ck]

    return pl.pallas_call(
        kern,
        out_shape=jax.ShapeDtypeStruct((nb, nh, lp), F32),
        grid=(nb,),
        in_specs=[pl.BlockSpec((1, nh, lp), lambda b: (b, 0, 0))],
        out_specs=pl.BlockSpec((1, nh, lp), lambda b: (b, 0, 0)),
        compiler_params=_params(1, 32),
        name="logf_cumsum",
    )(lt)


def _softmax_pv(parts, fq):
    m_keys = None
    for s, _ in parts:
        mx = jnp.max(s, axis=1, keepdims=True)
        m_keys = mx if m_keys is None else jnp.maximum(m_keys, mx)
    shift = fq - (m_keys + fq)
    l_sum, acc = None, None
    for s, v in parts:
        p = jnp.exp2(s + shift)
        ls = jnp.sum(p, axis=1, keepdims=True)
        pv = _dot(p.astype(BF16), v)
        l_sum = ls if l_sum is None else l_sum + ls
        acc = pv if acc is None else acc + pv
    return acc / l_sum


def _causal_mask(s):
    r = lax.broadcasted_iota(jnp.int32, s.shape, 0)
    c = lax.broadcasted_iota(jnp.int32, s.shape, 1)
    return jnp.where(c <= r, s, NEG_BIG)


def _attn_prompt_call(q, k, v, f_col, f_row, nb, seq, nh, head_dim):
    tq = _pick(seq, (256, 128, 64, 32, 8))
    nq = seq // tq

    def kern(q_ref, k_ref, v_ref, fc_ref, fr_ref, o_ref, fq_ref):
        hh = pl.program_id(1)
        fall = fc_ref[...]
        lane = lax.broadcasted_iota(jnp.int32, fall.shape, 1)
        fq_ref[...] = jnp.sum(jnp.where(lane == hh, fall, 0.0), axis=1, keepdims=True) * LOG2E
        for qi in range(nq):
            lo, hi = qi * tq, (qi + 1) * tq
            qv = q_ref[lo:hi, :]
            s_d = _causal_mask(_dot_nt(qv, k_ref[lo:hi, :]) - fr_ref[0, 0, :, lo:hi] * LOG2E)
            parts = [(s_d, v_ref[lo:hi, :])]
            if qi > 0:
                s_o = _dot_nt(qv, k_ref[0:lo, :]) - fr_ref[0, 0, :, 0:lo] * LOG2E
                parts.append((s_o, v_ref[0:lo, :]))
            o_ref[lo:hi, :] = _softmax_pv(parts, fq_ref[lo:hi, :]).astype(BF16)

    return pl.pallas_call(
        kern,
        out_shape=jax.ShapeDtypeStruct((nb * seq, nh * head_dim), BF16),
        grid=(nb, nh),
        in_specs=[pl.BlockSpec((seq, head_dim), lambda b, h: (b, h)),
                  pl.BlockSpec((seq, head_dim), lambda b, h: (b, h)),
                  pl.BlockSpec((seq, head_dim), lambda b, h: (b, h)),
                  pl.BlockSpec((seq, nh), lambda b, h: (b, 0)),
                  pl.BlockSpec((1, 1, 1, seq), lambda b, h: (b, h, 0, 0))],
        out_specs=pl.BlockSpec((seq, head_dim), lambda b, h: (b, h)),
        scratch_shapes=[pltpu.VMEM((seq, 1), F32)],
        compiler_params=_params(2, 40),
        name="attn_prompt",
    )(q, k, v, f_col, f_row)


def _attn_sample_call(q, k_new, v_new, cache_k, cache_v, fq_col, f_cache_row, f_new_row,
                      nb, s_len, past, nh, head_dim):
    n_steps = nb * nh
    n_slots = 3
    n_parts = 2 if past % (2 * SUBLANES) == 0 else 1
    part = past // n_parts

    def kern(q_ref, kn_ref, vn_ref, fq_ref, fc_ref, fn_ref, ck_hbm, cv_hbm, o_ref, kbuf, vbuf, sem):
        step = pl.program_id(0) * nh + pl.program_id(1)
        slot = lax.rem(step, n_slots)

        def copies(s_idx, slot_idx):
            bb = s_idx // nh
            hh = lax.rem(s_idx, nh)
            out = []
            for j in range(n_parts):
                rs = pl.ds(j * part, part)
                out.append(pltpu.make_async_copy(ck_hbm.at[bb, rs, hh, :], kbuf.at[slot_idx, rs],
                                                 sem.at[0, slot_idx, j]))
                out.append(pltpu.make_async_copy(cv_hbm.at[bb, rs, hh, :], vbuf.at[slot_idx, rs],
                                                 sem.at[1, slot_idx, j]))
            return out

        @pl.when(step == 0)
        def _():
            for ahead in range(min(n_slots - 1, n_steps)):
                for cp in copies(step + ahead, ahead):
                    cp.start()

        @pl.when(step + (n_slots - 1) < n_steps)
        def _():
            for cp in copies(step + (n_slots - 1), lax.rem(step + (n_slots - 1), n_slots)):
                cp.start()

        for cp in copies(step, slot):
            cp.wait()

        qv = q_ref[...]
        s_c = _dot_nt(qv, kbuf[slot].astype(BF16)) - fc_ref[0, 0] * LOG2E
        s_n = _causal_mask(_dot_nt(qv, kn_ref[...]) - fn_ref[0, 0] * LOG2E)
        parts = [(s_c, vbuf[slot].astype(BF16)), (s_n, vn_ref[...])]
        o_ref[...] = _softmax_pv(parts, fq_ref[0, 0] * LOG2E).astype(BF16)

    return pl.pallas_call(
        kern,
        out_shape=jax.ShapeDtypeStruct((nb * s_len, nh * head_dim), BF16),
        grid=(nb, nh),
        in_specs=[pl.BlockSpec((s_len, head_dim), lambda b, h: (b, h)),
                  pl.BlockSpec((s_len, head_dim), lambda b, h: (b, h)),
                  pl.BlockSpec((s_len, head_dim), lambda b, h: (b, h)),
                  pl.BlockSpec((1, 1, s_len, 1), lambda b, h: (b, h, 0, 0)),
                  pl.BlockSpec((1, 1, 1, past), lambda b, h: (b, h, 0, 0)),
                  pl.BlockSpec((1, 1, 1, s_len), lambda b, h: (b, h, 0, 0)),
                  pl.BlockSpec(memory_space=pl.ANY),
                  pl.BlockSpec(memory_space=pl.ANY)],
        out_specs=pl.BlockSpec((s_len, head_dim), lambda b, h: (b, h)),
        scratch_shapes=[pltpu.VMEM((n_slots, past, head_dim), F32),
                        pltpu.VMEM((n_slots, past, head_dim), F32),
                        pltpu.SemaphoreType.DMA((2, n_slots, n_parts))],
        compiler_params=_params(2, 40),
        name="attn_sample",
    )(q, k_new, v_new, fq_col, f_cache_row, f_new_row, cache_k, cache_v)


def _s5_call(name, u_tm, wb, wc, a_pack, d_row, h0_re, h0_im, nb, t_len):
    rows, d_ssm = u_tm.shape
    n_gt = wb.shape[0]
    cw = wb.shape[1]
    sw = wb.shape[2] // 2
    nt = 2 if n_gt % 2 == 0 else 1
    tc = _pick(t_len, tuple(c for c in (64, 32, 16, 8, 4, 2, 1) if c * nb <= 1024))
    n_chunks = t_len // tc
    rc = tc * nb
    assert nb % SUBLANES == 0

    def kern(u_ref, wb_ref, wc_ref, a_ref, d_ref, h0r_ref, h0i_ref, y_ref, hr_out, hi_out,
             hr_ref, hi_ref, bu_ref):
        c = pl.program_id(1)

        @pl.when(c == 0)
        def _():
            hr_ref[...] = h0r_ref[...]
            hi_ref[...] = h0i_ref[...]

        for g in range(nt):
            bu_ref[g] = _dot(u_ref[:, g * cw:(g + 1) * cw].astype(BF16), wb_ref[g])
        for g in range(nt):
            ar = jnp.broadcast_to(a_ref[g, 0:1, :], (SUBLANES, sw))
            ai = jnp.broadcast_to(a_ref[g, 1:2, :], (SUBLANES, sw))
            for r in range(nb // SUBLANES):
                rs = slice(r * SUBLANES, (r + 1) * SUBLANES)
                hr = hr_ref[rs, g * sw:(g + 1) * sw]
                hi = hi_ref[rs, g * sw:(g + 1) * sw]
                for t in range(tc):
                    ts = slice(t * nb + r * SUBLANES, t * nb + (r + 1) * SUBLANES)
                    nhr = ar * hr - ai * hi + bu_ref[g, ts, 0:sw]
                    nhi = ar * hi + ai * hr + bu_ref[g, ts, sw:2 * sw]
                    bu_ref[g, ts, 0:sw] = nhr
                    bu_ref[g, ts, sw:2 * sw] = nhi
                    hr, hi = nhr, nhi
                hr_ref[rs, g * sw:(g + 1) * sw] = hr
                hi_ref[rs, g * sw:(g + 1) * sw] = hi
            cs = slice(g * cw, (g + 1) * cw)
            y = _dot(bu_ref[g].astype(BF16), wc_ref[g]) + d_ref[:, cs] * u_ref[:, cs]
            y_ref[:, cs] = jax.nn.gelu(y, approximate=True).astype(BF16)

        @pl.when(c == n_chunks - 1)
        def _():
            hr_out[...] = hr_ref[...]
            hi_out[...] = hi_ref[...]

    return pl.pallas_call(
        kern,
        out_shape=[jax.ShapeDtypeStruct((rows, d_ssm), BF16),
                   jax.ShapeDtypeStruct((nb, n_gt * sw), F32),
                   jax.ShapeDtypeStruct((nb, n_gt * sw), F32)],
        grid=(n_gt // nt, n_chunks),
        in_specs=[pl.BlockSpec((rc, nt * cw), lambda g, c: (c, g)),
                  pl.BlockSpec((nt, cw, 2 * sw), lambda g, c: (g, 0, 0)),
                  pl.BlockSpec((nt, 2 * sw, cw), lambda g, c: (g, 0, 0)),
                  pl.BlockSpec((nt, 2, sw), lambda g, c: (g, 0, 0)),
                  pl.BlockSpec((1, nt * cw), lambda g, c: (0, g)),
                  pl.BlockSpec((nb, nt * sw), lambda g, c: (0, g)),
                  pl.BlockSpec((nb, nt * sw), lambda g, c: (0, g))],
        out_specs=[pl.BlockSpec((rc, nt * cw), lambda g, c: (c, g)),
                   pl.BlockSpec((nb, nt * sw), lambda g, c: (0, g)),
                   pl.BlockSpec((nb, nt * sw), lambda g, c: (0, g))],
        scratch_shapes=[pltpu.VMEM((nb, nt * sw), F32),
                        pltpu.VMEM((nb, nt * sw), F32),
                        pltpu.VMEM((nt, rc, 2 * sw), F32)],
        compiler_params=_params(2, 40),
        name=name,
    )(u_tm, wb, wc, a_pack, d_row, h0_re, h0_im)


def _s5_discretize(a_re, a_im, log_dt, b_re, b_im, c_re, c_im):
    g, p = a_re.shape
    ch = b_re.shape[2]
    gpt = S5_GROUPS_PER_TILE
    n_gt = g // gpt
    dt = jnp.exp(log_dt.astype(F32))[:, None]
    mag = jnp.exp(a_re * dt)
    abar_re = mag * jnp.cos(a_im * dt)
    abar_im = mag * jnp.sin(a_im * dt)
    xr = abar_re - 1
    den = a_re * a_re + a_im * a_im
    coef_re = ((xr * a_re + abar_im * a_im) / den)[..., None]
    coef_im = ((abar_im * a_re - xr * a_im) / den)[..., None]
    bb_re = coef_re * b_re - coef_im * b_im
    bb_im = coef_re * b_im + coef_im * b_re
    eye = jnp.eye(gpt, dtype=F32)

    def blockdiag_in(bb):
        t = bb.reshape(n_gt, gpt, p, ch)
        return jnp.einsum("tgpc,gh->tgchp", t, eye).reshape(n_gt, gpt * ch, gpt * p)

    def blockdiag_out(cc):
        t = cc.reshape(n_gt, gpt, ch, p)
        return jnp.einsum("tgcp,gh->tgphc", t, eye).reshape(n_gt, gpt * p, gpt * ch)

    wb = jnp.concatenate([blockdiag_in(bb_re), blockdiag_in(bb_im)], axis=2).astype(BF16)
    wc = jnp.concatenate([blockdiag_out(c_re), -blockdiag_out(c_im)], axis=1).astype(BF16)
    a_pack = jnp.stack([abar_re.reshape(n_gt, gpt * p), abar_im.reshape(n_gt, gpt * p)], axis=1)
    return wb, wc, a_pack


def _merge_call(h, o_attn, y_ssm, w_gate_bf, b_gate, w_attn_out_bf, w_glu_bf):
    m, d = h.shape
    bm = _pick(m, (512, 256, 128, 64, 32, 8))
    bn = _pick(d, (256, 128))
    nblk = d // bn

    def body(a, w, e, o):
        hv = a[0][...]
        ga = jax.nn.sigmoid(_dot(hv, w[0][...]) + e[0][...])
        gs = jax.nn.sigmoid(_dot(hv, w[1][...]) + e[1][...])
        ab = _dot(a[1][...], w[2][...])
        yv = a[2][...]
        sb = _dot(yv, w[3][...]) * jax.nn.sigmoid(_dot(yv, w[4][...]))
        o[0][...] = (ga * ab + gs * sb).astype(BF16)

    bg = b_gate.reshape(1, 2 * d)
    extras = [(bg, (1, bn), lambda i, j: (0, j)),
              (bg, (1, bn), lambda i, j: (0, j + nblk))]
    rhs = [(w_gate_bf, 0), (w_gate_bf, nblk), (w_attn_out_bf, 0), (w_glu_bf, 0), (w_glu_bf, nblk)]
    (merged,) = _mm_call("gated_merge", m, d, bm, bn, [h, o_attn, y_ssm], rhs, extras, [BF16], body,
                         vmem_mib=56)
    return merged


def _pad_time(x, lp):
    return jnp.pad(x, [(0, 0)] * (x.ndim - 1) + [(0, lp - x.shape[-1])])


def _layer(x, mod, wts, dims, cache=None, ssm_h0=None):
    nb, t_len, d = x.shape
    nh, head_dim, n_groups, n_state = dims
    d_attn = nh * head_dim
    sample = cache is not None
    m = nb * t_len
    if sample:
        xt = x.transpose(1, 0, 2).reshape(m, d)
        per, rpg = nb, m
        modg = [mod[:, j][None] for j in range(N_MOD)]
    else:
        xt = x.reshape(m, d)
        per, rpg = SUBLANES, t_len
        modg = [jnp.broadcast_to(mod[:, j][:, None, :], (nb, per, d)) for j in range(N_MOD)]

    h = _norm_mod_call(xt, wts["norm_g"][0], modg[1], modg[0], per, rpg)
    act = _ffn_in_call(h, wts["ffn1_in"])
    xt = _mm_res_call("ffn1_out", act, wts["ffn1_out"], xt, modg[2], HALF_STEP, per, rpg)

    h = _norm_mod_call(xt, wts["norm_g"][1], modg[4], modg[3], per, rpg)
    w_qkv = wts["w_qkv"]
    (q,) = _proj_call("proj_q", h, w_qkv, 0, d_attn, "q", gain=wts["q_norm_g"], head_dim=head_dim,
                      q_scale=head_dim ** -0.5 * LOG2E)
    k32, kbf = _proj_call("proj_k", h, w_qkv, d_attn, d_attn, "k", gain=wts["k_norm_g"], head_dim=head_dim)
    v32, vbf = _proj_call("proj_v", h, w_qkv, 2 * d_attn, d_attn, "v")
    (u,) = _proj_call("proj_u", h, wts["w_u"], 0, wts["w_u"].shape[1], "u")
    (lf,) = _proj_call("proj_f", h, wts["w_f"], 0, LANES, "f", bias=wts["b_f"])
    logf = lf[:, :nh]

    ck = CUMSUM_CHUNK
    if sample:
        cache_k, cache_v, cache_logf = cache
        past = cache_k.shape[1]
        logf_bt = logf.reshape(t_len, nb, nh).transpose(1, 0, 2)
        lt = jnp.concatenate([cache_logf.astype(F32), logf_bt], axis=1).transpose(0, 2, 1)
        lp = -(-(past + t_len) // ck) * ck
        f_row = _cumsum_call(_pad_time(lt, lp))
        f_cache_row = f_row[:, :, None, :past]
        f_new_row = f_row[:, :, None, past:past + t_len]
        fq_col = f_row[:, :, past:past + t_len, None]
        to_bt = lambda a: a.reshape(t_len, nb, -1).transpose(1, 0, 2).reshape(m, -1)
        o_bt = _attn_sample_call(to_bt(q), to_bt(kbf), to_bt(vbf), cache_k, cache_v,
                                 fq_col, f_cache_row, f_new_row, nb, t_len, past, nh, head_dim)
        o_attn = o_bt.reshape(nb, t_len, d_attn).transpose(1, 0, 2).reshape(m, d_attn)
        k_out = to_bt(k32).reshape(nb, t_len, nh, head_dim)
        v_out = to_bt(v32).reshape(nb, t_len, nh, head_dim)
        logf_out = logf_bt
        u_tm = u
        h0_re = ssm_h0[0].astype(F32).reshape(nb, n_groups * n_state)
        h0_im = ssm_h0[1].astype(F32).reshape(nb, n_groups * n_state)
    else:
        lt = logf.reshape(nb, t_len, nh).transpose(0, 2, 1)
        lp = -(-t_len // ck) * ck
        f_row = _cumsum_call(_pad_time(lt, lp))[:, :, :t_len]
        f_col = f_row.transpose(0, 2, 1).reshape(m, nh)
        o_attn = _attn_prompt_call(q, kbf, vbf, f_col, f_row[:, :, None, :], nb, t_len, nh, head_dim)
        k_out = k32.reshape(nb, t_len, nh, head_dim)
        v_out = v32.reshape(nb, t_len, nh, head_dim)
        logf_out = logf.reshape(nb, t_len, nh)
        u_tm = u.reshape(nb, t_len, -1).transpose(1, 0, 2).reshape(m, -1)
        h0_re = jnp.zeros((nb, n_groups * n_state), F32)
        h0_im = h0_re

    y_tm, h_re, h_im = _s5_call("s5_sample" if sample else "s5_prompt", u_tm, wts["s5_wb"], wts["s5_wc"],
                                wts["s5_a"], wts["s5_d"], h0_re, h0_im, nb, t_len)
    if sample:
        y_ssm = y_tm
    else:
        y_ssm = y_tm.reshape(t_len, nb, -1).transpose(1, 0, 2).reshape(m, -1)

    merged = _merge_call(h, o_attn, y_ssm, wts["w_gate"], wts["b_gate"], wts["w_attn_out"], wts["w_glu"])
    xt = _mm_res_call("mix_out", merged, wts["w_out"], xt, modg[5], 1.0, per, rpg)

    h = _norm_mod_call(xt, wts["norm_g"][2], modg[7], modg[6], per, rpg)
    act = _ffn_in_call(h, wts["ffn2_in"])
    xt = _mm_res_call("ffn2_out", act, wts["ffn2_out"], xt, modg[8], HALF_STEP, per, rpg)

    if sample:
        y = xt.reshape(t_len, nb, d).transpose(1, 0, 2)
    else:
        y = xt.reshape(nb, t_len, d)
    return (y, k_out, v_out, logf_out,
            h_re.reshape(nb, n_groups, n_state), h_im.reshape(nb, n_groups, n_state))


def kernel(x_prompt, x_sample, cache_k, cache_v, cache_logf, state_ssm_re, state_ssm_im, c_prompt, c_sample, w_ada, b_ada, norm_g, w_ffn1_in, w_ffn1_out, w_in, b_forget, q_norm_g, k_norm_g, w_attn_out, ssm_a_re, ssm_a_im, ssm_log_dt, ssm_b_re, ssm_b_im, ssm_c_re, ssm_c_im, ssm_d, w_glu, w_gate, b_gate, w_out, w_ffn2_in, w_ffn2_out):
    depth = w_ada.shape[0]
    nbp = x_prompt.shape[0]
    d = x_prompt.shape[2]
    nh, head_dim = cache_k.shape[3], cache_k.shape[4]
    n_groups, n_state, n_ch = ssm_b_re.shape[1:]
    d_attn = nh * head_dim
    d_ssm = n_groups * n_ch
    dims = (nh, head_dim, n_groups, n_state)
    assert head_dim == LANES and nh <= LANES
    assert S5_GROUPS_PER_TILE * n_ch == LANES and n_groups % S5_GROUPS_PER_TILE == 0

    xp, xs = x_prompt, x_sample
    outs_p, outs_s = [], []
    for l in range(depth):
        mod = _ada_call(jnp.concatenate([c_prompt, c_sample], axis=0), w_ada[l], b_ada[l])
        mod = mod.reshape(mod.shape[0], N_MOD, d)
        wb, wc, a_pack = _s5_discretize(ssm_a_re[l].astype(F32), ssm_a_im[l].astype(F32), ssm_log_dt[l],
                                        ssm_b_re[l].astype(F32), ssm_b_im[l].astype(F32),
                                        ssm_c_re[l].astype(F32), ssm_c_im[l].astype(F32))
        w_f = jnp.pad(w_in[l][:, 3 * d_attn:3 * d_attn + nh], ((0, 0), (0, LANES - nh))).astype(BF16)
        b_f = jnp.pad(b_forget[l].astype(F32), (0, LANES - nh)).reshape(1, LANES)
        wts = {
            "norm_g": norm_g[l].astype(F32),
            "ffn1_in": w_ffn1_in[l], "ffn1_out": w_ffn1_out[l].astype(BF16),
            "ffn2_in": w_ffn2_in[l], "ffn2_out": w_ffn2_out[l].astype(BF16),
            "w_qkv": w_in[l][:, :3 * d_attn].astype(BF16),
            "w_f": w_f, "b_f": b_f,
            "w_u": w_in[l][:, 3 * d_attn + nh:].astype(BF16),
            "q_norm_g": q_norm_g[l].astype(F32), "k_norm_g": k_norm_g[l].astype(F32),
            "w_attn_out": w_attn_out[l].astype(BF16), "w_glu": w_glu[l].astype(BF16),
            "w_gate": w_gate[l].astype(BF16), "b_gate": b_gate[l].astype(F32),
            "w_out": w_out[l].astype(BF16),
            "s5_wb": wb, "s5_wc": wc, "s5_a": a_pack, "s5_d": ssm_d[l].astype(F32).reshape(1, d_ssm),
        }
        xp, k1, v1, f1, r1, i1 = _layer(xp, mod[:nbp], wts, dims)
        xs, k2, v2, f2, r2, i2 = _layer(xs, mod[nbp:], wts, dims,
                                        cache=(cache_k[l], cache_v[l], cache_logf[l]),
                                        ssm_h0=(state_ssm_re[l], state_ssm_im[l]))
        outs_p.append((k1, v1, f1, r1, i1))
        outs_s.append((k2, v2, f2, r2, i2))

    stack = lambda outs, idx: jnp.stack([o[idx] for o in outs])
    return (xp, xs,
            stack(outs_p, 0), stack(outs_p, 1), stack(outs_p, 2), stack(outs_p, 3), stack(outs_p, 4),
            stack(outs_s, 0), stack(outs_s, 1), stack(outs_s, 2), stack(outs_s, 3), stack(outs_s, 4))
```

```python
import functools
import math

import jax
import jax.numpy as jnp
from jax import lax
from jax.experimental import pallas as pl
from jax.experimental.pallas import tpu as pltpu

F32 = jnp.float32
BF16 = jnp.bfloat16

EPS = 1e-6
HALF_STEP = 0.5
N_MOD = 9
NEG_BIG = -1e30
LOG2E = math.log2(math.e)
LANES = 128
SUBLANES = 8
MIB = 1024 * 1024
CUMSUM_CHUNK = 256
S5_GROUPS_PER_TILE = 8


def _params(n_axes, vmem_mib):
    return pltpu.CompilerParams(
        dimension_semantics=("arbitrary",) * n_axes,
        vmem_limit_bytes=vmem_mib * MIB,
    )


def _dot(a, b):
    return jnp.dot(a, b, preferred_element_type=F32)


def _dot_nt(a, b):
    return lax.dot_general(a, b, (((1,), (1,)), ((), ())), preferred_element_type=F32)


def _row_periodic(acc, per, fn):
    m, n = acc.shape
    return fn(acc.reshape(m // per, per, n)).reshape(m, n)


def _mm_call(name, m, n, bm, bn, lhs, rhs, extras, out_dtypes, body, vmem_mib=48, lhs_buffers=2,
             row_outs=()):
    assert m % bm == 0 and n % bn == 0, (name, m, n, bm, bn)
    in_specs = []
    for a in lhs:
        mode = {} if lhs_buffers == 2 else {"pipeline_mode": pl.Buffered(lhs_buffers)}
        in_specs.append(pl.BlockSpec((bm, a.shape[1]), lambda i, j: (i, 0), **mode))
    for w, off in rhs:
        in_specs.append(pl.BlockSpec((w.shape[0], bn), lambda i, j, off=off: (0, j + off)))
    for _, bs, im in extras:
        in_specs.append(pl.BlockSpec(bs, im))
    out_specs = [pl.BlockSpec((bm, bn), lambda i, j: (i, j)) for _ in out_dtypes]
    out_shape = [jax.ShapeDtypeStruct((m, n), dt) for dt in out_dtypes]
    for cols, dt in row_outs:
        out_specs.append(pl.BlockSpec((bm, cols), lambda i, j: (i, 0)))
        out_shape.append(jax.ShapeDtypeStruct((m, cols), dt))
    nl, nr, ne = len(lhs), len(rhs), len(extras)

    def kern(*refs):
        body(refs[:nl], refs[nl:nl + nr], refs[nl + nr:nl + nr + ne], refs[nl + nr + ne:])

    outs = pl.pallas_call(
        kern,
        out_shape=out_shape,
        grid=(m // bm, n // bn),
        in_specs=in_specs,
        out_specs=out_specs,
        compiler_params=_params(2, vmem_mib),
        name=name,
    )(*lhs, *[w for w, _ in rhs], *[e for e, _, _ in extras])
    return outs


def _pick(total, prefs):
    for p in prefs:
        if total % p == 0:
            return p
    return total


def _ada_call(c, w_ada, b_ada):
    mb, d = c.shape
    n = w_ada.shape[1]
    bn = _pick(n, (512, 256, 128))

    def kern(c_ref, w_ref, b_ref, o_ref):
        o_ref[...] = _dot(c_ref[...].astype(BF16), w_ref[...].astype(BF16)) + b_ref[...]

    return pl.pallas_call(
        kern,
        out_shape=jax.ShapeDtypeStruct((mb, n), F32),
        grid=(n // bn,),
        in_specs=[pl.BlockSpec((mb, d), lambda j: (0, 0)),
                  pl.BlockSpec((d, bn), lambda j: (0, j)),
                  pl.BlockSpec((1, bn), lambda j: (0, j))],
        out_specs=pl.BlockSpec((mb, bn), lambda j: (0, j)),
        compiler_params=_params(1, 40),
        name="ada_mod",
    )(c, w_ada, b_ada.reshape(1, n))


def _norm_mod_call(x, g, scale, shift, per, rows_per_group):
    m, d = x.shape
    bm = _pick(m, (256, 128, 64, 32, 8))
    bm = min(bm, rows_per_group)

    rows = SUBLANES
    assert per % rows == 0 and bm % per == 0

    def kern(x_ref, g_ref, sc_ref, sh_ref, o_ref):
        for c in range(bm // rows):
            rs = slice(c * rows, (c + 1) * rows)
            ps = slice((c * rows) % per, (c * rows) % per + rows)
            xv = x_ref[rs, :]
            ms = jnp.mean(xv * xv, axis=-1, keepdims=True)
            y = xv * lax.rsqrt(ms + EPS) * g_ref[...]
            y = y * (1.0 + sc_ref[0, ps, :]) + sh_ref[0, ps, :]
            o_ref[rs, :] = y.astype(BF16)

    grp = lambda i: (i * bm // rows_per_group, 0, 0)
    return pl.pallas_call(
        kern,
        out_shape=jax.ShapeDtypeStruct((m, d), BF16),
        grid=(m // bm,),
        in_specs=[pl.BlockSpec((bm, d), lambda i: (i, 0)),
                  pl.BlockSpec((1, d), lambda i: (0, 0)),
                  pl.BlockSpec((1, per, d), grp),
                  pl.BlockSpec((1, per, d), grp)],
        out_specs=pl.BlockSpec((bm, d), lambda i: (i, 0)),
        compiler_params=_params(1, 40),
        name="norm_mod",
    )(x, g.reshape(1, d), scale, shift)


def _ffn_in_call(h, w_in):
    m, d = h.shape
    f = w_in.shape[1] // 2
    bm = _pick(m, (2048, 1024, 512, 256, 128, 64, 32, 8))
    bn = _pick(f, (256, 128))

    def body(a, w, e, o):
        hv = a[0][...]
        gate = _dot(hv, w[0][...].astype(BF16))
        up = _dot(hv, w[1][...].astype(BF16))
        o[0][...] = (gate * jax.nn.sigmoid(gate) * up).astype(BF16)

    (act,) = _mm_call("ffn_in", m, f, bm, bn, [h], [(w_in, 0), (w_in, f // bn)], [],
                      [BF16], body, vmem_mib=56, lhs_buffers=1)
    return act


def _mm_res_call(name, a, w_bf, x, gate, coef, per, rows_per_group):
    m, k = a.shape
    n = w_bf.shape[1]
    bm = _pick(m, (1024, 512, 256, 128, 64, 32, 8) if k <= 4096 else (512, 256, 128, 64, 32, 8))
    bm = min(bm, rows_per_group)
    bn = _pick(n, (512, 256, 128))

    def body(lhs, rhs, e, o):
        acc = _dot(lhs[0][...], rhs[0][...])
        r = _row_periodic(acc, per, lambda a3: a3 * (coef * (1.0 + e[1][...])))
        o[0][...] = e[0][...] + r

    extras = [(x, (bm, bn), lambda i, j: (i, j)),
              (gate, (1, per, bn), lambda i, j: (i * bm // rows_per_group, 0, j))]
    (out,) = _mm_call(name, m, n, bm, bn, [a], [(w_bf, 0)], extras, [F32], body, vmem_mib=56)
    return out


def _head_rms(acc, gain, head_dim):
    outs = []
    for hh in range(acc.shape[1] // head_dim):
        y = acc[:, hh * head_dim:(hh + 1) * head_dim]
        ms = jnp.mean(y * y, axis=-1, keepdims=True)
        outs.append(y * lax.rsqrt(ms + EPS) * gain)
    return outs


def _log_sigmoid(z):
    return jnp.minimum(z, 0.0) - jnp.log1p(jnp.exp(-jnp.abs(z)))


def _proj_call(name, h, w_bf, col_off, n, mode, gain=None, head_dim=LANES, q_scale=1.0, w_f=None, b_f=None):
    m, k = h.shape
    bm = _pick(m, (1024, 512, 256, 128, 64, 32, 8))
    bn = _pick(n, (512, 256, 128))
    extras, row_outs = [], []
    if mode in ("q", "k"):
        extras.append((gain.reshape(1, head_dim), (1, head_dim), lambda i, j: (0, 0)))
    if mode == "v":
        extras.append((w_f, w_f.shape, lambda i, j: (0, 0)))
        extras.append((b_f, b_f.shape, lambda i, j: (0, 0)))
        row_outs.append((w_f.shape[1], F32))
    out_dtypes = {"q": [BF16], "k": [F32, BF16], "v": [F32, BF16], "u": [F32]}[mode]

    def body(a, w, e, o):
        acc = _dot(a[0][...], w[0][...])
        if mode in ("q", "k"):
            parts = _head_rms(acc, e[0][...], head_dim)
            for hh, y in enumerate(parts):
                sl = slice(hh * head_dim, (hh + 1) * head_dim)
                if mode == "q":
                    o[0][:, sl] = (y * q_scale).astype(BF16)
                else:
                    o[0][:, sl] = y
                    o[1][:, sl] = y.astype(BF16)
        elif mode == "v":
            o[0][...] = acc
            o[1][...] = acc.astype(BF16)

            @pl.when(pl.program_id(1) == 0)
            def _():
                o[2][...] = _log_sigmoid(_dot(a[0][...], e[0][...]) + e[1][...])
        else:
            o[0][...] = acc

    assert col_off % bn == 0
    return _mm_call(name, m, n, bm, bn, [h], [(w_bf, col_off // bn)], extras, out_dtypes, body, vmem_mib=48,
                    row_outs=row_outs)


def _cumsum_call(lt):
    nb, nh, lp = lt.shape
    ck = CUMSUM_CHUNK
    assert lp % ck == 0

    def kern(x_ref, o_ref):
        row = lax.broadcasted_iota(jnp.int32, (ck, ck), 0)
        col = lax.broadcasted_iota(jnp.int32, (ck, ck), 1)
        upper = (row <= col).astype(F32)
        carry = jnp.zeros((nh, 1), F32)
        for c in range(lp // ck):
            xs = x_ref[0, :, c * ck:(c + 1) * ck]
            ys = jnp.dot(xs, upper, preferred_element_type=F32, precision=lax.Precision.HIGHEST) + carry
            o_ref[0, :, c * ck:(c + 1) * ck] = ys
            carry = ys[:, ck - 1:ck]

    return pl.pallas_call(
        kern,
        out_shape=jax.ShapeDtypeStruct((nb, nh, lp), F32),
        grid=(nb,),
        in_specs=[pl.BlockSpec((1, nh, lp), lambda b: (b, 0, 0))],
        out_specs=pl.BlockSpec((1, nh, lp), lambda b: (b, 0, 0)),
        compiler_params=_params(1, 32),
        name="logf_cumsum",
    )(lt)


def _softmax_pv(parts, fq):
    m_keys = None
    for s, _ in parts:
        mx = jnp.max(s, axis=1, keepdims=True)
        m_keys = mx if m_keys is None else jnp.maximum(m_keys, mx)
    shift = fq - (m_keys + fq)
    l_sum, acc = None, None
    for s, v in parts:
        p = jnp.exp2(s + shift)
        ls = jnp.sum(p, axis=1, keepdims=True)
        pv = _dot(p.astype(BF16), v)
        l_sum = ls if l_sum is None else l_sum + ls
        acc = pv if acc is None else acc + pv
    return acc / l_sum


def _causal_mask(s):
    r = lax.broadcasted_iota(jnp.int32, s.shape, 0)
    c = lax.broadcasted_iota(jnp.int32, s.shape, 1)
    return jnp.where(c <= r, s, NEG_BIG)


def _attn_prompt_call(q, k, v, f_col, f_row, nb, seq, nh, head_dim):
    tq = _pick(seq, (256, 128, 64, 32, 8))
    nq = seq // tq
    hp = 2 if nh % 2 == 0 else 1
    hw = hp * head_dim

    def kern(q_ref, k_ref, v_ref, fc_ref, fr_ref, o_ref, fq_ref):
        fall = fc_ref[...]
        lane = lax.broadcasted_iota(jnp.int32, fall.shape, 1)
        for j in range(hp):
            hh = pl.program_id(1) * hp + j
            fq_ref[j] = jnp.sum(jnp.where(lane == hh, fall, 0.0), axis=1, keepdims=True) * LOG2E
        for qi in range(nq):
            lo, hi = qi * tq, (qi + 1) * tq
            for j in range(hp):
                cs = slice(j * head_dim, (j + 1) * head_dim)
                qv = q_ref[lo:hi, cs]
                s_d = _causal_mask(_dot_nt(qv, k_ref[lo:hi, cs]) - fr_ref[0, j, :, lo:hi] * LOG2E)
                parts = [(s_d, v_ref[lo:hi, cs])]
                if qi > 0:
                    s_o = _dot_nt(qv, k_ref[0:lo, cs]) - fr_ref[0, j, :, 0:lo] * LOG2E
                    parts.append((s_o, v_ref[0:lo, cs]))
                o_ref[lo:hi, cs] = _softmax_pv(parts, fq_ref[j, lo:hi, :]).astype(BF16)

    return pl.pallas_call(
        kern,
        out_shape=jax.ShapeDtypeStruct((nb * seq, nh * head_dim), BF16),
        grid=(nb, nh // hp),
        in_specs=[pl.BlockSpec((seq, hw), lambda b, h: (b, h)),
                  pl.BlockSpec((seq, hw), lambda b, h: (b, h)),
                  pl.BlockSpec((seq, hw), lambda b, h: (b, h)),
                  pl.BlockSpec((seq, nh), lambda b, h: (b, 0)),
                  pl.BlockSpec((1, hp, 1, seq), lambda b, h: (b, h, 0, 0))],
        out_specs=pl.BlockSpec((seq, hw), lambda b, h: (b, h)),
        scratch_shapes=[pltpu.VMEM((hp, seq, 1), F32)],
        compiler_params=_params(2, 40),
        name="attn_prompt",
    )(q, k, v, f_col, f_row)


def _attn_sample_call(q, k_new, v_new, cache_k, cache_v, fq_col, f_cache_row, f_new_row,
                      nb, s_len, past, nh, head_dim):
    n_steps = nb * nh
    n_slots = 3
    n_parts = 2 if past % (2 * SUBLANES) == 0 else 1
    part = past // n_parts

    def kern(q_ref, kn_ref, vn_ref, fq_ref, fc_ref, fn_ref, ck_hbm, cv_hbm, o_ref, kbuf, vbuf, sem):
        step = pl.program_id(0) * nh + pl.program_id(1)
        slot = lax.rem(step, n_slots)

        def copies(s_idx, slot_idx):
            bb = s_idx // nh
            hh = lax.rem(s_idx, nh)
            out = []
            for j in range(n_parts):
                rs = pl.ds(j * part, part)
                out.append(pltpu.make_async_copy(ck_hbm.at[bb, rs, hh, :], kbuf.at[slot_idx, rs],
                                                 sem.at[0, slot_idx, j]))
                out.append(pltpu.make_async_copy(cv_hbm.at[bb, rs, hh, :], vbuf.at[slot_idx, rs],
                                                 sem.at[1, slot_idx, j]))
            return out

        @pl.when(step == 0)
        def _():
            for ahead in range(min(n_slots - 1, n_steps)):
                for cp in copies(step + ahead, ahead):
                    cp.start()

        @pl.when(step + (n_slots - 1) < n_steps)
        def _():
            for cp in copies(step + (n_slots - 1), lax.rem(step + (n_slots - 1), n_slots)):
                cp.start()

        for cp in copies(step, slot):
            cp.wait()

        qv = q_ref[...]
        s_c = _dot_nt(qv, kbuf[slot].astype(BF16)) - fc_ref[0, 0] * LOG2E
        s_n = _causal_mask(_dot_nt(qv, kn_ref[...]) - fn_ref[0, 0] * LOG2E)
        parts = [(s_c, vbuf[slot].astype(BF16)), (s_n, vn_ref[...])]
        o_ref[...] = _softmax_pv(parts, fq_ref[0, 0] * LOG2E).astype(BF16)

    return pl.pallas_call(
        kern,
        out_shape=jax.ShapeDtypeStruct((nb * s_len, nh * head_dim), BF16),
        grid=(nb, nh),
        in_specs=[pl.BlockSpec((s_len, head_dim), lambda b, h: (b, h)),
                  pl.BlockSpec((s_len, head_dim), lambda b, h: (b, h)),
                  pl.BlockSpec((s_len, head_dim), lambda b, h: (b, h)),
                  pl.BlockSpec((1, 1, s_len, 1), lambda b, h: (b, h, 0, 0)),
                  pl.BlockSpec((1, 1, 1, past), lambda b, h: (b, h, 0, 0)),
                  pl.BlockSpec((1, 1, 1, s_len), lambda b, h: (b, h, 0, 0)),
                  pl.BlockSpec(memory_space=pl.ANY),
                  pl.BlockSpec(memory_space=pl.ANY)],
        out_specs=pl.BlockSpec((s_len, head_dim), lambda b, h: (b, h)),
        scratch_shapes=[pltpu.VMEM((n_slots, past, head_dim), F32),
                        pltpu.VMEM((n_slots, past, head_dim), F32),
                        pltpu.SemaphoreType.DMA((2, n_slots, n_parts))],
        compiler_params=_params(2, 40),
        name="attn_sample",
    )(q, k_new, v_new, fq_col, f_cache_row, f_new_row, cache_k, cache_v)


def _s5_call(name, u_tm, wb, wc, a_pack, d_row, h0_re, h0_im, nb, t_len):
    rows, d_ssm = u_tm.shape
    n_gt = wb.shape[0]
    cw = wb.shape[1]
    sw = wb.shape[2] // 2
    nt = 2 if n_gt % 2 == 0 else 1
    tc = _pick(t_len, tuple(c for c in (64, 32, 16, 8, 4, 2, 1) if c * nb <= 1024))
    n_chunks = t_len // tc
    rc = tc * nb
    assert nb % SUBLANES == 0

    def kern(u_ref, wb_ref, wc_ref, a_ref, d_ref, h0r_ref, h0i_ref, y_ref, hr_out, hi_out,
             hr_ref, hi_ref, bu_ref):
        c = pl.program_id(1)

        @pl.when(c == 0)
        def _():
            hr_ref[...] = h0r_ref[...]
            hi_ref[...] = h0i_ref[...]

        for g in range(nt):
            bu_ref[g] = _dot(u_ref[:, g * cw:(g + 1) * cw].astype(BF16), wb_ref[g])
        for g in range(nt):
            ar = jnp.broadcast_to(a_ref[g, 0:1, :], (SUBLANES, sw))
            ai = jnp.broadcast_to(a_ref[g, 1:2, :], (SUBLANES, sw))
            for r in range(nb // SUBLANES):
                rs = slice(r * SUBLANES, (r + 1) * SUBLANES)
                hr = hr_ref[rs, g * sw:(g + 1) * sw]
                hi = hi_ref[rs, g * sw:(g + 1) * sw]
                for t in range(tc):
                    ts = slice(t * nb + r * SUBLANES, t * nb + (r + 1) * SUBLANES)
                    nhr = ar * hr - ai * hi + bu_ref[g, ts, 0:sw]
                    nhi = ar * hi + ai * hr + bu_ref[g, ts, sw:2 * sw]
                    bu_ref[g, ts, 0:sw] = nhr
                    bu_ref[g, ts, sw:2 * sw] = nhi
                    hr, hi = nhr, nhi
                hr_ref[rs, g * sw:(g + 1) * sw] = hr
                hi_ref[rs, g * sw:(g + 1) * sw] = hi
            cs = slice(g * cw, (g + 1) * cw)
            y = _dot(bu_ref[g].astype(BF16), wc_ref[g]) + d_ref[:, cs] * u_ref[:, cs]
            y_ref[:, cs] = jax.nn.gelu(y, approximate=True).astype(BF16)

        @pl.when(c == n_chunks - 1)
        def _():
            hr_out[...] = hr_ref[...]
            hi_out[...] = hi_ref[...]

    return pl.pallas_call(
        kern,
        out_shape=[jax.ShapeDtypeStruct((rows, d_ssm), BF16),
                   jax.ShapeDtypeStruct((nb, n_gt * sw), F32),
                   jax.ShapeDtypeStruct((nb, n_gt * sw), F32)],
        grid=(n_gt // nt, n_chunks),
        in_specs=[pl.BlockSpec((rc, nt * cw), lambda g, c: (c, g)),
                  pl.BlockSpec((nt, cw, 2 * sw), lambda g, c: (g, 0, 0)),
                  pl.BlockSpec((nt, 2 * sw, cw), lambda g, c: (g, 0, 0)),
                  pl.BlockSpec((nt, 2, sw), lambda g, c: (g, 0, 0)),
                  pl.BlockSpec((1, nt * cw), lambda g, c: (0, g)),
                  pl.BlockSpec((nb, nt * sw), lambda g, c: (0, g)),
                  pl.BlockSpec((nb, nt * sw), lambda g, c: (0, g))],
        out_specs=[pl.BlockSpec((rc, nt * cw), lambda g, c: (c, g)),
                   pl.BlockSpec((nb, nt * sw), lambda g, c: (0, g)),
                   pl.BlockSpec((nb, nt * sw), lambda g, c: (0, g))],
        scratch_shapes=[pltpu.VMEM((nb, nt * sw), F32),
                        pltpu.VMEM((nb, nt * sw), F32),
                        pltpu.VMEM((nt, rc, 2 * sw), F32)],
        compiler_params=_params(2, 40),
        name=name,
    )(u_tm, wb, wc, a_pack, d_row, h0_re, h0_im)


def _s5_discretize(a_re, a_im, log_dt, b_re, b_im, c_re, c_im):
    g, p = a_re.shape
    ch = b_re.shape[2]
    gpt = S5_GROUPS_PER_TILE
    n_gt = g // gpt
    dt = jnp.exp(log_dt.astype(F32))[:, None]
    mag = jnp.exp(a_re * dt)
    abar_re = mag * jnp.cos(a_im * dt)
    abar_im = mag * jnp.sin(a_im * dt)
    xr = abar_re - 1
    den = a_re * a_re + a_im * a_im
    coef_re = ((xr * a_re + abar_im * a_im) / den)[..., None]
    coef_im = ((abar_im * a_re - xr * a_im) / den)[..., None]
    bb_re = coef_re * b_re - coef_im * b_im
    bb_im = coef_re * b_im + coef_im * b_re
    eye = jnp.eye(gpt, dtype=F32)

    def blockdiag_in(bb):
        t = bb.reshape(n_gt, gpt, p, ch)
        return jnp.einsum("tgpc,gh->tgchp", t, eye).reshape(n_gt, gpt * ch, gpt * p)

    def blockdiag_out(cc):
        t = cc.reshape(n_gt, gpt, ch, p)
        return jnp.einsum("tgcp,gh->tgphc", t, eye).reshape(n_gt, gpt * p, gpt * ch)

    wb = jnp.concatenate([blockdiag_in(bb_re), blockdiag_in(bb_im)], axis=2).astype(BF16)
    wc = jnp.concatenate([blockdiag_out(c_re), -blockdiag_out(c_im)], axis=1).astype(BF16)
    a_pack = jnp.stack([abar_re.reshape(n_gt, gpt * p), abar_im.reshape(n_gt, gpt * p)], axis=1)
    return wb, wc, a_pack


def _merge_call(h, o_attn, y_ssm, w_gate_bf, b_gate, w_attn_out_bf, w_glu_bf):
    m, d = h.shape
    bm = _pick(m, (512, 256, 128, 64, 32, 8))
    bn = _pick(d, (256, 128))
    nblk = d // bn

    def body(a, w, e, o):
        hv = a[0][...]
        ga = jax.nn.sigmoid(_dot(hv, w[0][...]) + e[0][...])
        gs = jax.nn.sigmoid(_dot(hv, w[1][...]) + e[1][...])
        ab = _dot(a[1][...], w[2][...])
        yv = a[2][...]
        sb = _dot(yv, w[3][...]) * jax.nn.sigmoid(_dot(yv, w[4][...]))
        o[0][...] = (ga * ab + gs * sb).astype(BF16)

    bg = b_gate.reshape(1, 2 * d)
    extras = [(bg, (1, bn), lambda i, j: (0, j)),
              (bg, (1, bn), lambda i, j: (0, j + nblk))]
    rhs = [(w_gate_bf, 0), (w_gate_bf, nblk), (w_attn_out_bf, 0), (w_glu_bf, 0), (w_glu_bf, nblk)]
    (merged,) = _mm_call("gated_merge", m, d, bm, bn, [h, o_attn, y_ssm], rhs, extras, [BF16], body,
                         vmem_mib=56)
    return merged


def _pad_time(x, lp):
    return jnp.pad(x, [(0, 0)] * (x.ndim - 1) + [(0, lp - x.shape[-1])])


def _layer(x, mod, wts, dims, cache=None, ssm_h0=None):
    nb, t_len, d = x.shape
    nh, head_dim, n_groups, n_state = dims
    d_attn = nh * head_dim
    sample = cache is not None
    m = nb * t_len
    if sample:
        xt = x.transpose(1, 0, 2).reshape(m, d)
        per, rpg = nb, m
        modg = [mod[:, j][None] for j in range(N_MOD)]
    else:
        xt = x.reshape(m, d)
        per, rpg = SUBLANES, t_len
        modg = [jnp.broadcast_to(mod[:, j][:, None, :], (nb, per, d)) for j in range(N_MOD)]

    h = _norm_mod_call(xt, wts["norm_g"][0], modg[1], modg[0], per, rpg)
    act = _ffn_in_call(h, wts["ffn1_in"])
    xt = _mm_res_call("ffn1_out", act, wts["ffn1_out"], xt, modg[2], HALF_STEP, per, rpg)

    h = _norm_mod_call(xt, wts["norm_g"][1], modg[4], modg[3], per, rpg)
    w_in_bf = wts["w_in"]
    (q,) = _proj_call("proj_q", h, w_in_bf, 0, d_attn, "q", gain=wts["q_norm_g"], head_dim=head_dim,
                      q_scale=head_dim ** -0.5 * LOG2E)
    k32, kbf = _proj_call("proj_k", h, w_in_bf, d_attn, d_attn, "k", gain=wts["k_norm_g"], head_dim=head_dim)
    v32, vbf, lf = _proj_call("proj_v", h, w_in_bf, 2 * d_attn, d_attn, "v", w_f=wts["w_f"], b_f=wts["b_f"])
    (u,) = _proj_call("proj_u", h, wts["w_u"], 0, wts["w_u"].shape[1], "u")
    logf = lf[:, :nh]

    ck = CUMSUM_CHUNK
    if sample:
        cache_k, cache_v, cache_logf = cache
        past = cache_k.shape[1]
        logf_bt = logf.reshape(t_len, nb, nh).transpose(1, 0, 2)
        lt = jnp.concatenate([cache_logf.astype(F32), logf_bt], axis=1).transpose(0, 2, 1)
        lp = -(-(past + t_len) // ck) * ck
        f_row = _cumsum_call(_pad_time(lt, lp))
        f_cache_row = f_row[:, :, None, :past]
        f_new_row = f_row[:, :, None, past:past + t_len]
        fq_col = f_row[:, :, past:past + t_len, None]
        to_bt = lambda a: a.reshape(t_len, nb, -1).transpose(1, 0, 2).reshape(m, -1)
        o_bt = _attn_sample_call(to_bt(q), to_bt(kbf), to_bt(vbf), cache_k, cache_v,
                                 fq_col, f_cache_row, f_new_row, nb, t_len, past, nh, head_dim)
        o_attn = o_bt.reshape(nb, t_len, d_attn).transpose(1, 0, 2).reshape(m, d_attn)
        k_out = to_bt(k32).reshape(nb, t_len, nh, head_dim)
        v_out = to_bt(v32).reshape(nb, t_len, nh, head_dim)
        logf_out = logf_bt
        u_tm = u
        h0_re = ssm_h0[0].astype(F32).reshape(nb, n_groups * n_state)
        h0_im = ssm_h0[1].astype(F32).reshape(nb, n_groups * n_state)
    else:
        lt = logf.reshape(nb, t_len, nh).transpose(0, 2, 1)
        lp = -(-t_len // ck) * ck
        f_row = _cumsum_call(_pad_time(lt, lp))[:, :, :t_len]
        f_col = f_row.transpose(0, 2, 1).reshape(m, nh)
        o_attn = _attn_prompt_call(q, kbf, vbf, f_col, f_row[:, :, None, :], nb, t_len, nh, head_dim)
        k_out = k32.reshape(nb, t_len, nh, head_dim)
        v_out = v32.reshape(nb, t_len, nh, head_dim)
        logf_out = logf.reshape(nb, t_len, nh)
        u_tm = u.reshape(nb, t_len, -1).transpose(1, 0, 2).reshape(m, -1)
        h0_re = jnp.zeros((nb, n_groups * n_state), F32)
        h0_im = h0_re

    y_tm, h_re, h_im = _s5_call("s5_sample" if sample else "s5_prompt", u_tm, wts["s5_wb"], wts["s5_wc"],
                                wts["s5_a"], wts["s5_d"], h0_re, h0_im, nb, t_len)
    if sample:
        y_ssm = y_tm
    else:
        y_ssm = y_tm.reshape(t_len, nb, -1).transpose(1, 0, 2).reshape(m, -1)

    merged = _merge_call(h, o_attn, y_ssm, wts["w_gate"], wts["b_gate"], wts["w_attn_out"], wts["w_glu"])
    xt = _mm_res_call("mix_out", merged, wts["w_out"], xt, modg[5], 1.0, per, rpg)

    h = _norm_mod_call(xt, wts["norm_g"][2], modg[7], modg[6], per, rpg)
    act = _ffn_in_call(h, wts["ffn2_in"])
    xt = _mm_res_call("ffn2_out", act, wts["ffn2_out"], xt, modg[8], HALF_STEP, per, rpg)

    if sample:
        y = xt.reshape(t_len, nb, d).transpose(1, 0, 2)
    else:
        y = xt.reshape(nb, t_len, d)
    return (y, k_out, v_out, logf_out,
            h_re.reshape(nb, n_groups, n_state), h_im.reshape(nb, n_groups, n_state))


def kernel(x_prompt, x_sample, cache_k, cache_v, cache_logf, state_ssm_re, state_ssm_im, c_prompt, c_sample, w_ada, b_ada, norm_g, w_ffn1_in, w_ffn1_out, w_in, b_forget, q_norm_g, k_norm_g, w_attn_out, ssm_a_re, ssm_a_im, ssm_log_dt, ssm_b_re, ssm_b_im, ssm_c_re, ssm_c_im, ssm_d, w_glu, w_gate, b_gate, w_out, w_ffn2_in, w_ffn2_out):
    depth = w_ada.shape[0]
    nbp = x_prompt.shape[0]
    d = x_prompt.shape[2]
    nh, head_dim = cache_k.shape[3], cache_k.shape[4]
    n_groups, n_state, n_ch = ssm_b_re.shape[1:]
    d_attn = nh * head_dim
    d_ssm = n_groups * n_ch
    dims = (nh, head_dim, n_groups, n_state)
    assert head_dim == LANES and nh <= LANES
    assert S5_GROUPS_PER_TILE * n_ch == LANES and n_groups % S5_GROUPS_PER_TILE == 0

    xp, xs = x_prompt, x_sample
    outs_p, outs_s = [], []
    for l in range(depth):
        mod = _ada_call(jnp.concatenate([c_prompt, c_sample], axis=0), w_ada[l], b_ada[l])
        mod = mod.reshape(mod.shape[0], N_MOD, d)
        wb, wc, a_pack = _s5_discretize(ssm_a_re[l].astype(F32), ssm_a_im[l].astype(F32), ssm_log_dt[l],
                                        ssm_b_re[l].astype(F32), ssm_b_im[l].astype(F32),
                                        ssm_c_re[l].astype(F32), ssm_c_im[l].astype(F32))
        w_in_bf = w_in[l].astype(BF16)
        w_f = jnp.pad(w_in_bf[:, 3 * d_attn:3 * d_attn + nh], ((0, 0), (0, LANES - nh)))
        b_f = jnp.pad(b_forget[l].astype(F32), (0, LANES - nh)).reshape(1, LANES)
        wts = {
            "norm_g": norm_g[l].astype(F32),
            "ffn1_in": w_ffn1_in[l], "ffn1_out": w_ffn1_out[l].astype(BF16),
            "ffn2_in": w_ffn2_in[l], "ffn2_out": w_ffn2_out[l].astype(BF16),
            "w_in": w_in_bf, "w_f": w_f, "b_f": b_f,
            "w_u": w_in_bf[:, 3 * d_attn + nh:],
            "q_norm_g": q_norm_g[l].astype(F32), "k_norm_g": k_norm_g[l].astype(F32),
            "w_attn_out": w_attn_out[l].astype(BF16), "w_glu": w_glu[l].astype(BF16),
            "w_gate": w_gate[l].astype(BF16), "b_gate": b_gate[l].astype(F32),
            "w_out": w_out[l].astype(BF16),
            "s5_wb": wb, "s5_wc": wc, "s5_a": a_pack, "s5_d": ssm_d[l].astype(F32).reshape(1, d_ssm),
        }
        xp, k1, v1, f1, r1, i1 = _layer(xp, mod[:nbp], wts, dims)
        xs, k2, v2, f2, r2, i2 = _layer(xs, mod[nbp:], wts, dims,
                                        cache=(cache_k[l], cache_v[l], cache_logf[l]),
                                        ssm_h0=(state_ssm_re[l], state_ssm_im[l]))
        outs_p.append((k1, v1, f1, r1, i1))
        outs_s.append((k2, v2, f2, r2, i2))

    stack = lambda outs, idx: jnp.stack([o[idx] for o in outs])
    return (xp, xs,
            stack(outs_p, 0), stack(outs_p, 1), stack(outs_p, 2), stack(outs_p, 3), stack(outs_p, 4),
            stack(outs_s, 0), stack(outs_s, 1), stack(outs_s, 2), stack(outs_s, 3), stack(outs_s, 4))
```

```python
import functools
import math

import jax
import jax.numpy as jnp
from jax import lax
from jax.experimental import pallas as pl
from jax.experimental.pallas import tpu as pltpu

F32 = jnp.float32
BF16 = jnp.bfloat16

EPS = 1e-6
HALF_STEP = 0.5
N_MOD = 9
NEG_BIG = -1e30
LOG2E = math.log2(math.e)
LANES = 128
SUBLANES = 8
MIB = 1024 * 1024
CUMSUM_CHUNK = 256
S5_GROUPS_PER_TILE = 8


def _params(n_axes, vmem_mib):
    return pltpu.CompilerParams(
        dimension_semantics=("arbitrary",) * n_axes,
        vmem_limit_bytes=vmem_mib * MIB,
    )


def _dot(a, b):
    return jnp.dot(a, b, preferred_element_type=F32)


def _dot_nt(a, b):
    return lax.dot_general(a, b, (((1,), (1,)), ((), ())), preferred_element_type=F32)


def _row_periodic(acc, per, fn):
    m, n = acc.shape
    return fn(acc.reshape(m // per, per, n)).reshape(m, n)


def _cast_rows(r, steps):
    tile = 2 * SUBLANES
    for rows in range(tile, r + 1, tile):
        if r % rows == 0 and r // rows <= steps:
            return rows
    return None


def _mm_call(name, m, n, bm, bn, lhs, rhs, extras, out_dtypes, body, vmem_mib=48, lhs_buffers=2,
             extra_outs=(), side_casts=()):
    assert m % bm == 0 and n % bn == 0, (name, m, n, bm, bn)
    nj = n // bn
    steps = (m // bm) * nj
    in_specs = []
    for a in lhs:
        mode = {} if lhs_buffers == 2 else {"pipeline_mode": pl.Buffered(lhs_buffers)}
        in_specs.append(pl.BlockSpec((bm, a.shape[1]), lambda i, j: (i, 0), **mode))
    for w, off in rhs:
        in_specs.append(pl.BlockSpec((w.shape[0], bn), lambda i, j, off=off: (0, j + off)))
    for _, bs, im in extras:
        in_specs.append(pl.BlockSpec(bs, im))
    out_specs = [pl.BlockSpec((bm, bn), lambda i, j: (i, j)) for _ in out_dtypes]
    out_shape = [jax.ShapeDtypeStruct((m, n), dt) for dt in out_dtypes]
    for shape, dt, bs, im in extra_outs:
        out_specs.append(pl.BlockSpec(bs, im))
        out_shape.append(jax.ShapeDtypeStruct(shape, dt))
    hosted = [arr for arr in side_casts if _cast_rows(arr.shape[0], steps) is not None]
    for arr in hosted:
        r, c = arr.shape
        rows = _cast_rows(r, steps)
        im = lambda i, j, last=r // rows - 1: (jnp.minimum(i * nj + j, last), 0)
        in_specs.append(pl.BlockSpec((rows, c), im))
        out_specs.append(pl.BlockSpec((rows, c), im))
        out_shape.append(jax.ShapeDtypeStruct((r, c), BF16))
    nl, nr, ne, ns = len(lhs), len(rhs), len(extras), len(hosted)
    n_in = nl + nr + ne + ns
    n_out = len(out_dtypes) + len(extra_outs)

    def kern(*refs):
        body(refs[:nl], refs[nl:nl + nr], refs[nl + nr:nl + nr + ne], refs[n_in:n_in + n_out])
        for s in range(ns):
            refs[n_in + n_out + s][...] = refs[nl + nr + ne + s][...].astype(BF16)

    outs = pl.pallas_call(
        kern,
        out_shape=out_shape,
        grid=(m // bm, nj),
        in_specs=in_specs,
        out_specs=out_specs,
        compiler_params=_params(2, vmem_mib),
        name=name,
    )(*lhs, *[w for w, _ in rhs], *[e for e, _, _ in extras], *hosted)
    outs = list(outs)
    main, cast_done = outs[:n_out], iter(outs[n_out:])
    casts = [next(cast_done) if any(arr is h_ for h_ in hosted) else arr.astype(BF16) for arr in side_casts]
    return main + casts


def _pick(total, prefs):
    for p in prefs:
        if total % p == 0:
            return p
    return total


def _ada_call(c, w_ada, b_ada, col0, n_cols):
    mb, d = c.shape
    bn = math.gcd(_pick(n_cols, (512, 256, 128)), col0) if col0 else _pick(n_cols, (512, 256, 128))
    assert bn % LANES == 0 and n_cols % bn == 0
    off = col0 // bn

    def kern(c_ref, w_ref, b_ref, o_ref):
        o_ref[...] = _dot(c_ref[...].astype(BF16), w_ref[...].astype(BF16)) + b_ref[...]

    return pl.pallas_call(
        kern,
        out_shape=jax.ShapeDtypeStruct((mb, n_cols), F32),
        grid=(n_cols // bn,),
        in_specs=[pl.BlockSpec((mb, d), lambda j: (0, 0)),
                  pl.BlockSpec((d, bn), lambda j: (0, j + off)),
                  pl.BlockSpec((1, bn), lambda j: (0, j + off))],
        out_specs=pl.BlockSpec((mb, bn), lambda j: (0, j)),
        compiler_params=_params(1, 40),
        name="ada_mod",
    )(c, w_ada, b_ada)


def _norm_mod_call(x, g, scale, shift, per, rows_per_group):
    m, d = x.shape
    bm = _pick(m, (256, 128, 64, 32, 8))
    bm = min(bm, rows_per_group)

    rows = SUBLANES
    assert per % rows == 0 and bm % per == 0

    def kern(x_ref, g_ref, sc_ref, sh_ref, o_ref):
        for c in range(bm // rows):
            rs = slice(c * rows, (c + 1) * rows)
            ps = slice((c * rows) % per, (c * rows) % per + rows)
            xv = x_ref[rs, :]
            ms = jnp.mean(xv * xv, axis=-1, keepdims=True)
            y = xv * lax.rsqrt(ms + EPS) * g_ref[...]
            y = y * (1.0 + sc_ref[0, ps, :]) + sh_ref[0, ps, :]
            o_ref[rs, :] = y.astype(BF16)

    grp = lambda i: (i * bm // rows_per_group, 0, 0)
    return pl.pallas_call(
        kern,
        out_shape=jax.ShapeDtypeStruct((m, d), BF16),
        grid=(m // bm,),
        in_specs=[pl.BlockSpec((bm, d), lambda i: (i, 0)),
                  pl.BlockSpec((1, d), lambda i: (0, 0)),
                  pl.BlockSpec((1, per, d), grp),
                  pl.BlockSpec((1, per, d), grp)],
        out_specs=pl.BlockSpec((bm, d), lambda i: (i, 0)),
        compiler_params=_params(1, 40),
        name="norm_mod",
    )(x, g.reshape(1, d), scale, shift)


def _ffn_in_call(h, w_in, side_casts=(), ada=None):
    m, d = h.shape
    f = w_in.shape[1] // 2
    bm = _pick(m, (2048, 1024, 512, 256, 128, 64, 32, 8))
    bn = _pick(f, (256, 128))
    nj = f // bn
    steps = (m // bm) * nj
    extras, extra_outs = [], []
    ada_separate = None
    if ada is not None:
        c, w_ada, b_ada, col0 = ada
        n_rest = w_ada.shape[1] - col0
        assert col0 % LANES == 0 and n_rest % LANES == 0
        if n_rest // LANES > steps:
            ada_separate, ada = _ada_call(c, w_ada, b_ada, col0, n_rest), None
    if ada is not None:
        blk = lambda i, j: jnp.minimum(i * nj + j, n_rest // LANES - 1)
        extras = [(c, c.shape, lambda i, j: (0, 0)),
                  (w_ada, (w_ada.shape[0], LANES), lambda i, j: (0, col0 // LANES + blk(i, j))),
                  (b_ada, (1, LANES), lambda i, j: (0, col0 // LANES + blk(i, j)))]
        extra_outs = [((c.shape[0], n_rest), F32, (c.shape[0], LANES), lambda i, j: (0, blk(i, j)))]

    def body(a, w, e, o):
        hv = a[0][...]
        gate = _dot(hv, w[0][...].astype(BF16))
        up = _dot(hv, w[1][...].astype(BF16))
        o[0][...] = (gate * jax.nn.sigmoid(gate) * up).astype(BF16)
        if ada is not None:
            o[1][...] = _dot(e[0][...].astype(BF16), e[1][...].astype(BF16)) + e[2][...]

    outs = _mm_call("ffn_in", m, f, bm, bn, [h], [(w_in, 0), (w_in, nj)], extras, [BF16], body, vmem_mib=56,
                    lhs_buffers=1, extra_outs=extra_outs, side_casts=side_casts)
    if ada is not None:
        outs = [outs[0]] + outs[2:] + [outs[1]]
    elif ada_separate is not None:
        outs = outs + [ada_separate]
    return outs


def _mm_res_call(name, a, w_bf, x, gate, coef, per, rows_per_group):
    m, k = a.shape
    n = w_bf.shape[1]
    bm = _pick(m, (1024, 512, 256, 128, 64, 32, 8) if k <= 4096 else (512, 256, 128, 64, 32, 8))
    bm = min(bm, rows_per_group)
    bn = _pick(n, (512, 256, 128))

    def body(lhs, rhs, e, o):
        acc = _dot(lhs[0][...], rhs[0][...])
        r = _row_periodic(acc, per, lambda a3: a3 * (coef * (1.0 + e[1][...])))
        o[0][...] = e[0][...] + r

    extras = [(x, (bm, bn), lambda i, j: (i, j)),
              (gate, (1, per, bn), lambda i, j: (i * bm // rows_per_group, 0, j))]
    (out,) = _mm_call(name, m, n, bm, bn, [a], [(w_bf, 0)], extras, [F32], body, vmem_mib=56)
    return out


def _head_rms(acc, gain, head_dim):
    outs = []
    for hh in range(acc.shape[1] // head_dim):
        y = acc[:, hh * head_dim:(hh + 1) * head_dim]
        ms = jnp.mean(y * y, axis=-1, keepdims=True)
        outs.append(y * lax.rsqrt(ms + EPS) * gain)
    return outs


def _log_sigmoid(z):
    return jnp.minimum(z, 0.0) - jnp.log1p(jnp.exp(-jnp.abs(z)))


def _proj_call(name, h, w_bf, col_off, n, mode, gain=None, head_dim=LANES, q_scale=1.0, w_f=None, b_f=None,
               side_casts=()):
    m, k = h.shape
    bm = _pick(m, (1024, 512, 256, 128, 64, 32, 8))
    bn = _pick(n, (512, 256, 128))
    extras, extra_outs = [], []
    if mode in ("q", "k"):
        extras.append((gain.reshape(1, head_dim), (1, head_dim), lambda i, j: (0, 0)))
    if mode == "v":
        extras.append((w_f, w_f.shape, lambda i, j: (0, 0)))
        extras.append((b_f, b_f.shape, lambda i, j: (0, 0)))
        extra_outs.append(((m, w_f.shape[1]), F32, (bm, w_f.shape[1]), lambda i, j: (i, 0)))
    out_dtypes = {"q": [BF16], "k": [F32, BF16], "v": [F32, BF16], "u": [F32]}[mode]

    def body(a, w, e, o):
        acc = _dot(a[0][...], w[0][...])
        if mode in ("q", "k"):
            parts = _head_rms(acc, e[0][...], head_dim)
            for hh, y in enumerate(parts):
                sl = slice(hh * head_dim, (hh + 1) * head_dim)
                if mode == "q":
                    o[0][:, sl] = (y * q_scale).astype(BF16)
                else:
                    o[0][:, sl] = y
                    o[1][:, sl] = y.astype(BF16)
        elif mode == "v":
            o[0][...] = acc
            o[1][...] = acc.astype(BF16)

            @pl.when(pl.program_id(1) == 0)
            def _():
                o[2][...] = _log_sigmoid(_dot(a[0][...], e[0][...]) + e[1][...])
        else:
            o[0][...] = acc

    assert col_off % bn == 0
    return _mm_call(name, m, n, bm, bn, [h], [(w_bf, col_off // bn)], extras, out_dtypes, body, vmem_mib=48,
                    extra_outs=extra_outs, side_casts=side_casts)


def _cumsum_call(lt):
    nb, nh, lp = lt.shape
    ck = CUMSUM_CHUNK
    assert lp % ck == 0

    def kern(x_ref, o_ref):
        row = lax.broadcasted_iota(jnp.int32, (ck, ck), 0)
        col = lax.broadcasted_iota(jnp.int32, (ck, ck), 1)
        upper = (row <= col).astype(F32)
        carry = jnp.zeros((nh, 1), F32)
        for c in range(lp // ck):
            xs = x_ref[0, :, c * ck:(c + 1) * ck]
            ys = jnp.dot(xs, upper, preferred_element_type=F32, precision=lax.Precision.HIGHEST) + carry
            o_ref[0, :, c * ck:(c + 1) * ck] = ys
            carry = ys[:, ck - 1:ck]

    return pl.pallas_call(
        kern,
        out_shape=jax.ShapeDtypeStruct((nb, nh, lp), F32),
        grid=(nb,),
        in_specs=[pl.BlockSpec((1, nh, lp), lambda b: (b, 0, 0))],
        out_specs=pl.BlockSpec((1, nh, lp), lambda b: (b, 0, 0)),
        compiler_params=_params(1, 32),
        name="logf_cumsum",
    )(lt)


def _softmax_pv(parts, fq):
    m_keys = None
    for s, _ in parts:
        mx = jnp.max(s, axis=1, keepdims=True)
        m_keys = mx if m_keys is None else jnp.maximum(m_keys, mx)
    shift = fq - (m_keys + fq)
    l_sum, acc = None, None
    for s, v in parts:
        p = jnp.exp2(s + shift)
        ls = jnp.sum(p, axis=1, keepdims=True)
        pv = _dot(p.astype(BF16), v)
        l_sum = ls if l_sum is None else l_sum + ls
        acc = pv if acc is None else acc + pv
    return acc / l_sum


def _causal_mask(s):
    r = lax.broadcasted_iota(jnp.int32, s.shape, 0)
    c = lax.broadcasted_iota(jnp.int32, s.shape, 1)
    return jnp.where(c <= r, s, NEG_BIG)


def _attn_prompt_call(q, k, v, f_col, f_row, nb, seq, nh, head_dim):
    tq = _pick(seq, (256, 128, 64, 32, 8))
    nq = seq // tq
    hp = 2 if nh % 2 == 0 else 1
    hw = hp * head_dim

    def kern(q_ref, k_ref, v_ref, fc_ref, fr_ref, o_ref, fq_ref):
        fall = fc_ref[...]
        lane = lax.broadcasted_iota(jnp.int32, fall.shape, 1)
        for j in range(hp):
            hh = pl.program_id(1) * hp + j
            fq_ref[j] = jnp.sum(jnp.where(lane == hh, fall, 0.0), axis=1, keepdims=True) * LOG2E
        for qi in range(nq):
            lo, hi = qi * tq, (qi + 1) * tq
            for j in range(hp):
                cs = slice(j * head_dim, (j + 1) * head_dim)
                qv = q_ref[lo:hi, cs]
                s_d = _causal_mask(_dot_nt(qv, k_ref[lo:hi, cs]) - fr_ref[0, j, :, lo:hi] * LOG2E)
                parts = [(s_d, v_ref[lo:hi, cs])]
                if qi > 0:
                    s_o = _dot_nt(qv, k_ref[0:lo, cs]) - fr_ref[0, j, :, 0:lo] * LOG2E
                    parts.append((s_o, v_ref[0:lo, cs]))
                o_ref[lo:hi, cs] = _softmax_pv(parts, fq_ref[j, lo:hi, :]).astype(BF16)

    return pl.pallas_call(
        kern,
        out_shape=jax.ShapeDtypeStruct((nb * seq, nh * head_dim), BF16),
        grid=(nb, nh // hp),
        in_specs=[pl.BlockSpec((seq, hw), lambda b, h: (b, h)),
                  pl.BlockSpec((seq, hw), lambda b, h: (b, h)),
                  pl.BlockSpec((seq, hw), lambda b, h: (b, h)),
                  pl.BlockSpec((seq, nh), lambda b, h: (b, 0)),
                  pl.BlockSpec((1, hp, 1, seq), lambda b, h: (b, h, 0, 0))],
        out_specs=pl.BlockSpec((seq, hw), lambda b, h: (b, h)),
        scratch_shapes=[pltpu.VMEM((hp, seq, 1), F32)],
        compiler_params=_params(2, 40),
        name="attn_prompt",
    )(q, k, v, f_col, f_row)


def _attn_sample_call(q, k_new, v_new, cache_k, cache_v, fq_col, f_cache_row, f_new_row,
                      nb, s_len, past, nh, head_dim):
    n_steps = nb * nh
    n_slots = 3
    n_parts = 2 if past % (2 * SUBLANES) == 0 else 1
    part = past // n_parts

    def kern(q_ref, kn_ref, vn_ref, fq_ref, fc_ref, fn_ref, ck_hbm, cv_hbm, o_ref, kbuf, vbuf, sem):
        step = pl.program_id(0) * nh + pl.program_id(1)
        slot = lax.rem(step, n_slots)

        def copies(s_idx, slot_idx):
            bb = s_idx // nh
            hh = lax.rem(s_idx, nh)
            out = []
            for j in range(n_parts):
                rs = pl.ds(j * part, part)
                out.append(pltpu.make_async_copy(ck_hbm.at[bb, rs, hh, :], kbuf.at[slot_idx, rs],
                                                 sem.at[0, slot_idx, j]))
                out.append(pltpu.make_async_copy(cv_hbm.at[bb, rs, hh, :], vbuf.at[slot_idx, rs],
                                                 sem.at[1, slot_idx, j]))
            return out

        @pl.when(step == 0)
        def _():
            for ahead in range(min(n_slots - 1, n_steps)):
                for cp in copies(step + ahead, ahead):
                    cp.start()

        @pl.when(step + (n_slots - 1) < n_steps)
        def _():
            for cp in copies(step + (n_slots - 1), lax.rem(step + (n_slots - 1), n_slots)):
                cp.start()

        for cp in copies(step, slot):
            cp.wait()

        qv = q_ref[...]
        s_c = _dot_nt(qv, kbuf[slot].astype(BF16)) - fc_ref[0, 0] * LOG2E
        s_n = _causal_mask(_dot_nt(qv, kn_ref[...]) - fn_ref[0, 0] * LOG2E)
        parts = [(s_c, vbuf[slot].astype(BF16)), (s_n, vn_ref[...])]
        o_ref[...] = _softmax_pv(parts, fq_ref[0, 0] * LOG2E).astype(BF16)

    return pl.pallas_call(
        kern,
        out_shape=jax.ShapeDtypeStruct((nb * s_len, nh * head_dim), BF16),
        grid=(nb, nh),
        in_specs=[pl.BlockSpec((s_len, head_dim), lambda b, h: (b, h)),
                  pl.BlockSpec((s_len, head_dim), lambda b, h: (b, h)),
                  pl.BlockSpec((s_len, head_dim), lambda b, h: (b, h)),
                  pl.BlockSpec((1, 1, s_len, 1), lambda b, h: (b, h, 0, 0)),
                  pl.BlockSpec((1, 1, 1, past), lambda b, h: (b, h, 0, 0)),
                  pl.BlockSpec((1, 1, 1, s_len), lambda b, h: (b, h, 0, 0)),
                  pl.BlockSpec(memory_space=pl.ANY),
                  pl.BlockSpec(memory_space=pl.ANY)],
        out_specs=pl.BlockSpec((s_len, head_dim), lambda b, h: (b, h)),
        scratch_shapes=[pltpu.VMEM((n_slots, past, head_dim), F32),
                        pltpu.VMEM((n_slots, past, head_dim), F32),
                        pltpu.SemaphoreType.DMA((2, n_slots, n_parts))],
        compiler_params=_params(2, 40),
        name="attn_sample",
    )(q, k_new, v_new, fq_col, f_cache_row, f_new_row, cache_k, cache_v)


def _s5_call(name, u_tm, wb, wc, a_pack, d_row, h0_re, h0_im, nb, t_len):
    rows, d_ssm = u_tm.shape
    n_gt = wb.shape[0]
    cw = wb.shape[1]
    sw = wb.shape[2] // 2
    nt = 2 if n_gt % 2 == 0 else 1
    tc = _pick(t_len, tuple(c for c in (64, 32, 16, 8, 4, 2, 1) if c * nb <= 1024))
    n_chunks = t_len // tc
    rc = tc * nb
    assert nb % SUBLANES == 0

    def kern(u_ref, wb_ref, wc_ref, a_ref, d_ref, h0r_ref, h0i_ref, y_ref, hr_out, hi_out,
             hr_ref, hi_ref, bu_ref):
        c = pl.program_id(1)

        @pl.when(c == 0)
        def _():
            hr_ref[...] = h0r_ref[...]
            hi_ref[...] = h0i_ref[...]

        for g in range(nt):
            bu_ref[g] = _dot(u_ref[:, g * cw:(g + 1) * cw].astype(BF16), wb_ref[g])
        for g in range(nt):
            ar = jnp.broadcast_to(a_ref[g, 0:1, :], (SUBLANES, sw))
            ai = jnp.broadcast_to(a_ref[g, 1:2, :], (SUBLANES, sw))
            for r in range(nb // SUBLANES):
                rs = slice(r * SUBLANES, (r + 1) * SUBLANES)
                hr = hr_ref[rs, g * sw:(g + 1) * sw]
                hi = hi_ref[rs, g * sw:(g + 1) * sw]
                for t in range(tc):
                    ts = slice(t * nb + r * SUBLANES, t * nb + (r + 1) * SUBLANES)
                    nhr = ar * hr - ai * hi + bu_ref[g, ts, 0:sw]
                    nhi = ar * hi + ai * hr + bu_ref[g, ts, sw:2 * sw]
                    bu_ref[g, ts, 0:sw] = nhr
                    bu_ref[g, ts, sw:2 * sw] = nhi
                    hr, hi = nhr, nhi
                hr_ref[rs, g * sw:(g + 1) * sw] = hr
                hi_ref[rs, g * sw:(g + 1) * sw] = hi
            cs = slice(g * cw, (g + 1) * cw)
            y = _dot(bu_ref[g].astype(BF16), wc_ref[g]) + d_ref[:, cs] * u_ref[:, cs]
            y_ref[:, cs] = jax.nn.gelu(y, approximate=True).astype(BF16)

        @pl.when(c == n_chunks - 1)
        def _():
            hr_out[...] = hr_ref[...]
            hi_out[...] = hi_ref[...]

    return pl.pallas_call(
        kern,
        out_shape=[jax.ShapeDtypeStruct((rows, d_ssm), BF16),
                   jax.ShapeDtypeStruct((nb, n_gt * sw), F32),
                   jax.ShapeDtypeStruct((nb, n_gt * sw), F32)],
        grid=(n_gt // nt, n_chunks),
        in_specs=[pl.BlockSpec((rc, nt * cw), lambda g, c: (c, g)),
                  pl.BlockSpec((nt, cw, 2 * sw), lambda g, c: (g, 0, 0)),
                  pl.BlockSpec((nt, 2 * sw, cw), lambda g, c: (g, 0, 0)),
                  pl.BlockSpec((nt, 2, sw), lambda g, c: (g, 0, 0)),
                  pl.BlockSpec((1, nt * cw), lambda g, c: (0, g)),
                  pl.BlockSpec((nb, nt * sw), lambda g, c: (0, g)),
                  pl.BlockSpec((nb, nt * sw), lambda g, c: (0, g))],
        out_specs=[pl.BlockSpec((rc, nt * cw), lambda g, c: (c, g)),
                   pl.BlockSpec((nb, nt * sw), lambda g, c: (0, g)),
                   pl.BlockSpec((nb, nt * sw), lambda g, c: (0, g))],
        scratch_shapes=[pltpu.VMEM((nb, nt * sw), F32),
                        pltpu.VMEM((nb, nt * sw), F32),
                        pltpu.VMEM((nt, rc, 2 * sw), F32)],
        compiler_params=_params(2, 40),
        name=name,
    )(u_tm, wb, wc, a_pack, d_row, h0_re, h0_im)


def _s5_discretize(a_re, a_im, log_dt, b_re, b_im, c_re, c_im):
    g, p = a_re.shape
    ch = b_re.shape[2]
    gpt = S5_GROUPS_PER_TILE
    n_gt = g // gpt
    dt = jnp.exp(log_dt.astype(F32))[:, None]
    mag = jnp.exp(a_re * dt)
    abar_re = mag * jnp.cos(a_im * dt)
    abar_im = mag * jnp.sin(a_im * dt)
    xr = abar_re - 1
    den = a_re * a_re + a_im * a_im
    coef_re = ((xr * a_re + abar_im * a_im) / den)[..., None]
    coef_im = ((abar_im * a_re - xr * a_im) / den)[..., None]
    bb_re = coef_re * b_re - coef_im * b_im
    bb_im = coef_re * b_im + coef_im * b_re
    eye = jnp.eye(gpt, dtype=F32)

    def blockdiag_in(bb):
        t = bb.reshape(n_gt, gpt, p, ch)
        return jnp.einsum("tgpc,gh->tgchp", t, eye).reshape(n_gt, gpt * ch, gpt * p)

    def blockdiag_out(cc):
        t = cc.reshape(n_gt, gpt, ch, p)
        return jnp.einsum("tgcp,gh->tgphc", t, eye).reshape(n_gt, gpt * p, gpt * ch)

    wb = jnp.concatenate([blockdiag_in(bb_re), blockdiag_in(bb_im)], axis=2).astype(BF16)
    wc = jnp.concatenate([blockdiag_out(c_re), -blockdiag_out(c_im)], axis=1).astype(BF16)
    a_pack = jnp.stack([abar_re.reshape(n_gt, gpt * p), abar_im.reshape(n_gt, gpt * p)], axis=1)
    return wb, wc, a_pack


def _merge_call(h, o_attn, y_ssm, w_gate_bf, b_gate, w_attn_out_bf, w_glu_bf):
    m, d = h.shape
    bm = _pick(m, (512, 256, 128, 64, 32, 8))
    bn = _pick(d, (256, 128))
    nblk = d // bn

    def body(a, w, e, o):
        hv = a[0][...]
        ga = jax.nn.sigmoid(_dot(hv, w[0][...]) + e[0][...])
        gs = jax.nn.sigmoid(_dot(hv, w[1][...]) + e[1][...])
        ab = _dot(a[1][...], w[2][...])
        yv = a[2][...]
        sb = _dot(yv, w[3][...]) * jax.nn.sigmoid(_dot(yv, w[4][...]))
        o[0][...] = (ga * ab + gs * sb).astype(BF16)

    bg = b_gate.reshape(1, 2 * d)
    extras = [(bg, (1, bn), lambda i, j: (0, j)),
              (bg, (1, bn), lambda i, j: (0, j + nblk))]
    rhs = [(w_gate_bf, 0), (w_gate_bf, nblk), (w_attn_out_bf, 0), (w_glu_bf, 0), (w_glu_bf, nblk)]
    (merged,) = _mm_call("gated_merge", m, d, bm, bn, [h, o_attn, y_ssm], rhs, extras, [BF16], body,
                         vmem_mib=56)
    return merged


def _pad_time(x, lp):
    return jnp.pad(x, [(0, 0)] * (x.ndim - 1) + [(0, lp - x.shape[-1])])


def _layer(x, mod_rows, shared, dims, cache=None, ssm_h0=None):
    nb, t_len, d = x.shape
    nh, head_dim, n_groups, n_state = dims
    d_attn = nh * head_dim
    sample = cache is not None
    m = nb * t_len
    wts = shared
    raw = shared.pop("raw", None)
    ada = shared.pop("ada", None)
    if sample:
        xt = x.transpose(1, 0, 2).reshape(m, d)
        per, rpg = nb, m
        modg = lambda j: shared["mod"][mod_rows, j][None]
    else:
        xt = x.reshape(m, d)
        per, rpg = SUBLANES, t_len
        modg = lambda j: jnp.broadcast_to(shared["mod"][mod_rows, j][:, None, :], (nb, per, d))
    side = (lambda *names: [raw[k] for k in names]) if raw is not None else (lambda *names: [])

    h = _norm_mod_call(xt, wts["norm_g"][0], modg(1), modg(0), per, rpg)
    outs = _ffn_in_call(h, wts["ffn1_in"], side_casts=side("ffn1_out", "w_in"), ada=ada)
    act = outs[0]
    if raw is not None:
        wts["ffn1_out"], w_in_bf = outs[1], outs[2]
        wts["w_in"] = w_in_bf
        wts["w_f"] = jnp.pad(w_in_bf[:, 3 * d_attn:3 * d_attn + nh], ((0, 0), (0, LANES - nh)))
        wts["w_u"] = w_in_bf[:, 3 * d_attn + nh:]
    if ada is not None:
        n_all = shared["mod"].shape[0]
        shared["mod"] = jnp.concatenate([shared["mod"].reshape(n_all, -1), outs[-1]], axis=1).reshape(n_all, N_MOD, d)
    xt = _mm_res_call("ffn1_out", act, wts["ffn1_out"], xt, modg(2), HALF_STEP, per, rpg)

    h = _norm_mod_call(xt, wts["norm_g"][1], modg(4), modg(3), per, rpg)
    w_in_bf = wts["w_in"]
    outs = _proj_call("proj_q", h, w_in_bf, 0, d_attn, "q", gain=wts["q_norm_g"], head_dim=head_dim,
                      q_scale=head_dim ** -0.5 * LOG2E, side_casts=side("w_gate"))
    q = outs[0]
    if raw is not None:
        wts["w_gate"] = outs[1]
    outs = _proj_call("proj_k", h, w_in_bf, d_attn, d_attn, "k", gain=wts["k_norm_g"], head_dim=head_dim,
                      side_casts=side("w_glu"))
    k32, kbf = outs[0], outs[1]
    if raw is not None:
        wts["w_glu"] = outs[2]
    outs = _proj_call("proj_v", h, w_in_bf, 2 * d_attn, d_attn, "v", w_f=wts["w_f"], b_f=wts["b_f"],
                      side_casts=side("w_attn_out"))
    v32, vbf, lf = outs[0], outs[1], outs[2]
    if raw is not None:
        wts["w_attn_out"] = outs[3]
    outs = _proj_call("proj_u", h, wts["w_u"], 0, wts["w_u"].shape[1], "u", side_casts=side("w_out"))
    u = outs[0]
    if raw is not None:
        wts["w_out"] = outs[1]
    logf = lf[:, :nh]

    ck = CUMSUM_CHUNK
    if sample:
        cache_k, cache_v, cache_logf = cache
        past = cache_k.shape[1]
        logf_bt = logf.reshape(t_len, nb, nh).transpose(1, 0, 2)
        lt = jnp.concatenate([cache_logf.astype(F32), logf_bt], axis=1).transpose(0, 2, 1)
        lp = -(-(past + t_len) // ck) * ck
        f_row = _cumsum_call(_pad_time(lt, lp))
        f_cache_row = f_row[:, :, None, :past]
        f_new_row = f_row[:, :, None, past:past + t_len]
        fq_col = f_row[:, :, past:past + t_len, None]
        to_bt = lambda a: a.reshape(t_len, nb, -1).transpose(1, 0, 2).reshape(m, -1)
        o_bt = _attn_sample_call(to_bt(q), to_bt(kbf), to_bt(vbf), cache_k, cache_v,
                                 fq_col, f_cache_row, f_new_row, nb, t_len, past, nh, head_dim)
        o_attn = o_bt.reshape(nb, t_len, d_attn).transpose(1, 0, 2).reshape(m, d_attn)
        k_out = to_bt(k32).reshape(nb, t_len, nh, head_dim)
        v_out = to_bt(v32).reshape(nb, t_len, nh, head_dim)
        logf_out = logf_bt
        u_tm = u
        h0_re = ssm_h0[0].astype(F32).reshape(nb, n_groups * n_state)
        h0_im = ssm_h0[1].astype(F32).reshape(nb, n_groups * n_state)
    else:
        lt = logf.reshape(nb, t_len, nh).transpose(0, 2, 1)
        lp = -(-t_len // ck) * ck
        f_row = _cumsum_call(_pad_time(lt, lp))[:, :, :t_len]
        f_col = f_row.transpose(0, 2, 1).reshape(m, nh)
        o_attn = _attn_prompt_call(q, kbf, vbf, f_col, f_row[:, :, None, :], nb, t_len, nh, head_dim)
        k_out = k32.reshape(nb, t_len, nh, head_dim)
        v_out = v32.reshape(nb, t_len, nh, head_dim)
        logf_out = logf.reshape(nb, t_len, nh)
        u_tm = u.reshape(nb, t_len, -1).transpose(1, 0, 2).reshape(m, -1)
        h0_re = jnp.zeros((nb, n_groups * n_state), F32)
        h0_im = h0_re

    y_tm, h_re, h_im = _s5_call("s5_sample" if sample else "s5_prompt", u_tm, wts["s5_wb"], wts["s5_wc"],
                                wts["s5_a"], wts["s5_d"], h0_re, h0_im, nb, t_len)
    if sample:
        y_ssm = y_tm
    else:
        y_ssm = y_tm.reshape(t_len, nb, -1).transpose(1, 0, 2).reshape(m, -1)

    merged = _merge_call(h, o_attn, y_ssm, wts["w_gate"], wts["b_gate"], wts["w_attn_out"], wts["w_glu"])
    xt = _mm_res_call("mix_out", merged, wts["w_out"], xt, modg(5), 1.0, per, rpg)

    h = _norm_mod_call(xt, wts["norm_g"][2], modg(7), modg(6), per, rpg)
    outs = _ffn_in_call(h, wts["ffn2_in"], side_casts=side("ffn2_out"))
    act = outs[0]
    if raw is not None:
        wts["ffn2_out"] = outs[1]
    xt = _mm_res_call("ffn2_out", act, wts["ffn2_out"], xt, modg(8), HALF_STEP, per, rpg)

    if sample:
        y = xt.reshape(t_len, nb, d).transpose(1, 0, 2)
    else:
        y = xt.reshape(nb, t_len, d)
    return (y, k_out, v_out, logf_out,
            h_re.reshape(nb, n_groups, n_state), h_im.reshape(nb, n_groups, n_state))


def kernel(x_prompt, x_sample, cache_k, cache_v, cache_logf, state_ssm_re, state_ssm_im, c_prompt, c_sample, w_ada, b_ada, norm_g, w_ffn1_in, w_ffn1_out, w_in, b_forget, q_norm_g, k_norm_g, w_attn_out, ssm_a_re, ssm_a_im, ssm_log_dt, ssm_b_re, ssm_b_im, ssm_c_re, ssm_c_im, ssm_d, w_glu, w_gate, b_gate, w_out, w_ffn2_in, w_ffn2_out):
    depth = w_ada.shape[0]
    nbp = x_prompt.shape[0]
    d = x_prompt.shape[2]
    nh, head_dim = cache_k.shape[3], cache_k.shape[4]
    n_groups, n_state, n_ch = ssm_b_re.shape[1:]
    d_attn = nh * head_dim
    d_ssm = n_groups * n_ch
    dims = (nh, head_dim, n_groups, n_state)
    assert head_dim == LANES and nh <= LANES
    assert S5_GROUPS_PER_TILE * n_ch == LANES and n_groups % S5_GROUPS_PER_TILE == 0

    xp, xs = x_prompt, x_sample
    outs_p, outs_s = [], []
    for l in range(depth):
        c_all = jnp.concatenate([c_prompt, c_sample], axis=0)
        b_ada_row = b_ada[l].reshape(1, -1)
        n_first = 2 * d
        mod_first = _ada_call(c_all, w_ada[l], b_ada_row, 0, n_first).reshape(c_all.shape[0], 2, d)
        wb, wc, a_pack = _s5_discretize(ssm_a_re[l].astype(F32), ssm_a_im[l].astype(F32), ssm_log_dt[l],
                                        ssm_b_re[l].astype(F32), ssm_b_im[l].astype(F32),
                                        ssm_c_re[l].astype(F32), ssm_c_im[l].astype(F32))
        shared = {
            "mod": mod_first,
            "ada": (c_all, w_ada[l], b_ada_row, n_first),
            "raw": {"ffn1_out": w_ffn1_out[l], "ffn2_out": w_ffn2_out[l], "w_in": w_in[l],
                    "w_gate": w_gate[l], "w_glu": w_glu[l], "w_attn_out": w_attn_out[l], "w_out": w_out[l]},
            "norm_g": norm_g[l].astype(F32),
            "ffn1_in": w_ffn1_in[l], "ffn2_in": w_ffn2_in[l],
            "b_f": jnp.pad(b_forget[l].astype(F32), (0, LANES - nh)).reshape(1, LANES),
            "q_norm_g": q_norm_g[l].astype(F32), "k_norm_g": k_norm_g[l].astype(F32),
            "b_gate": b_gate[l].astype(F32),
            "s5_wb": wb, "s5_wc": wc, "s5_a": a_pack, "s5_d": ssm_d[l].astype(F32).reshape(1, d_ssm),
        }
        xp, k1, v1, f1, r1, i1 = _layer(xp, slice(0, nbp), shared, dims)
        xs, k2, v2, f2, r2, i2 = _layer(xs, slice(nbp, None), shared, dims,
                                        cache=(cache_k[l], cache_v[l], cache_logf[l]),
                                        ssm_h0=(state_ssm_re[l], state_ssm_im[l]))
        outs_p.append((k1, v1, f1, r1, i1))
        outs_s.append((k2, v2, f2, r2, i2))

    stack = lambda outs, idx: jnp.stack([o[idx] for o in outs])
    return (xp, xs,
            stack(outs_p, 0), stack(outs_p, 1), stack(outs_p, 2), stack(outs_p, 3), stack(outs_p, 4),
            stack(outs_s, 0), stack(outs_s, 1), stack(outs_s, 2), stack(outs_s, 3), stack(outs_s, 4))
```

```python
import functools
import math

import jax
import jax.numpy as jnp
from jax import lax
from jax.experimental import pallas as pl
from jax.experimental.pallas import tpu as pltpu

F32 = jnp.float32
BF16 = jnp.bfloat16

EPS = 1e-6
HALF_STEP = 0.5
N_MOD = 9
NEG_BIG = -1e30
LOG2E = math.log2(math.e)
LANES = 128
SUBLANES = 8
MIB = 1024 * 1024
CUMSUM_CHUNK = 256
S5_GROUPS_PER_TILE = 8


def _params(n_axes, vmem_mib):
    return pltpu.CompilerParams(
        dimension_semantics=("arbitrary",) * n_axes,
        vmem_limit_bytes=vmem_mib * MIB,
    )


def _dot(a, b):
    return jnp.dot(a, b, preferred_element_type=F32)


def _dot_nt(a, b):
    return lax.dot_general(a, b, (((1,), (1,)), ((), ())), preferred_element_type=F32)


def _row_periodic(acc, per, fn):
    m, n = acc.shape
    return fn(acc.reshape(m // per, per, n)).reshape(m, n)


def _cast_rows(r, steps):
    tile = 2 * SUBLANES
    for rows in range(tile, r + 1, tile):
        if r % rows == 0 and r // rows <= steps:
            return rows
    return None


def _mm_call(name, m, n, bm, bn, lhs, rhs, extras, out_dtypes, body, vmem_mib=48, lhs_buffers=2,
             extra_outs=(), side_casts=()):
    assert m % bm == 0 and n % bn == 0, (name, m, n, bm, bn)
    nj = n // bn
    steps = (m // bm) * nj
    in_specs = []
    for a in lhs:
        mode = {} if lhs_buffers == 2 else {"pipeline_mode": pl.Buffered(lhs_buffers)}
        in_specs.append(pl.BlockSpec((bm, a.shape[1]), lambda i, j: (i, 0), **mode))
    for w, off in rhs:
        in_specs.append(pl.BlockSpec((w.shape[0], bn), lambda i, j, off=off: (0, j + off)))
    for _, bs, im in extras:
        in_specs.append(pl.BlockSpec(bs, im))
    out_specs = [pl.BlockSpec((bm, bn), lambda i, j: (i, j)) for _ in out_dtypes]
    out_shape = [jax.ShapeDtypeStruct((m, n), dt) for dt in out_dtypes]
    for shape, dt, bs, im in extra_outs:
        out_specs.append(pl.BlockSpec(bs, im))
        out_shape.append(jax.ShapeDtypeStruct(shape, dt))
    hosted = [arr for arr in side_casts if _cast_rows(arr.shape[0], steps) is not None]
    for arr in hosted:
        r, c = arr.shape
        rows = _cast_rows(r, steps)
        im = lambda i, j, last=r // rows - 1: (jnp.minimum(i * nj + j, last), 0)
        in_specs.append(pl.BlockSpec((rows, c), im))
        out_specs.append(pl.BlockSpec((rows, c), im))
        out_shape.append(jax.ShapeDtypeStruct((r, c), BF16))
    nl, nr, ne, ns = len(lhs), len(rhs), len(extras), len(hosted)
    n_in = nl + nr + ne + ns
    n_out = len(out_dtypes) + len(extra_outs)

    def kern(*refs):
        body(refs[:nl], refs[nl:nl + nr], refs[nl + nr:nl + nr + ne], refs[n_in:n_in + n_out])
        for s in range(ns):
            refs[n_in + n_out + s][...] = refs[nl + nr + ne + s][...].astype(BF16)

    outs = pl.pallas_call(
        kern,
        out_shape=out_shape,
        grid=(m // bm, nj),
        in_specs=in_specs,
        out_specs=out_specs,
        compiler_params=_params(2, vmem_mib),
        name=name,
    )(*lhs, *[w for w, _ in rhs], *[e for e, _, _ in extras], *hosted)
    outs = list(outs)
    main, cast_done = outs[:n_out], iter(outs[n_out:])
    casts = [next(cast_done) if any(arr is h_ for h_ in hosted) else arr.astype(BF16) for arr in side_casts]
    return main + casts


def _pick(total, prefs):
    for p in prefs:
        if total % p == 0:
            return p
    return total


def _ada_call(c, w_ada, b_ada, col0, n_cols):
    mb, d = c.shape
    bn = math.gcd(_pick(n_cols, (512, 256, 128)), col0) if col0 else _pick(n_cols, (512, 256, 128))
    assert bn % LANES == 0 and n_cols % bn == 0
    off = col0 // bn

    def kern(c_ref, w_ref, b_ref, o_ref):
        o_ref[...] = _dot(c_ref[...].astype(BF16), w_ref[...].astype(BF16)) + b_ref[...]

    return pl.pallas_call(
        kern,
        out_shape=jax.ShapeDtypeStruct((mb, n_cols), F32),
        grid=(n_cols // bn,),
        in_specs=[pl.BlockSpec((mb, d), lambda j: (0, 0)),
                  pl.BlockSpec((d, bn), lambda j: (0, j + off)),
                  pl.BlockSpec((1, bn), lambda j: (0, j + off))],
        out_specs=pl.BlockSpec((mb, bn), lambda j: (0, j)),
        compiler_params=_params(1, 40),
        name="ada_mod",
    )(c, w_ada, b_ada)


def _norm_mod_call(x, g, scale, shift, per, rows_per_group):
    m, d = x.shape
    bm = _pick(m, (256, 128, 64, 32, 8))
    bm = min(bm, rows_per_group)

    rows = SUBLANES
    assert per % rows == 0 and bm % per == 0

    def kern(x_ref, g_ref, sc_ref, sh_ref, o_ref):
        for c in range(bm // rows):
            rs = slice(c * rows, (c + 1) * rows)
            ps = slice((c * rows) % per, (c * rows) % per + rows)
            xv = x_ref[rs, :]
            ms = jnp.mean(xv * xv, axis=-1, keepdims=True)
            y = xv * lax.rsqrt(ms + EPS) * g_ref[...]
            y = y * (1.0 + sc_ref[0, ps, :]) + sh_ref[0, ps, :]
            o_ref[rs, :] = y.astype(BF16)

    grp = lambda i: (i * bm // rows_per_group, 0, 0)
    return pl.pallas_call(
        kern,
        out_shape=jax.ShapeDtypeStruct((m, d), BF16),
        grid=(m // bm,),
        in_specs=[pl.BlockSpec((bm, d), lambda i: (i, 0)),
                  pl.BlockSpec((1, d), lambda i: (0, 0)),
                  pl.BlockSpec((1, per, d), grp),
                  pl.BlockSpec((1, per, d), grp)],
        out_specs=pl.BlockSpec((bm, d), lambda i: (i, 0)),
        compiler_params=_params(1, 40),
        name="norm_mod",
    )(x, g.reshape(1, d), scale, shift)


def _ffn_in_call(h, w_in, side_casts=(), ada=None):
    m, d = h.shape
    f = w_in.shape[1] // 2
    bm = _pick(m, (2048, 1024, 512, 256, 128, 64, 32, 8))
    bn = _pick(f, (256, 128))
    nj = f // bn
    steps = (m // bm) * nj
    extras, extra_outs = [], []
    ada_separate = None
    if ada is not None:
        c, w_ada, b_ada, col0 = ada
        n_rest = w_ada.shape[1] - col0
        assert col0 % LANES == 0 and n_rest % LANES == 0
        if n_rest // LANES > steps:
            ada_separate, ada = _ada_call(c, w_ada, b_ada, col0, n_rest), None
    if ada is not None:
        blk = lambda i, j: jnp.minimum(i * nj + j, n_rest // LANES - 1)
        extras = [(c, c.shape, lambda i, j: (0, 0)),
                  (w_ada, (w_ada.shape[0], LANES), lambda i, j: (0, col0 // LANES + blk(i, j))),
                  (b_ada, (1, LANES), lambda i, j: (0, col0 // LANES + blk(i, j)))]
        extra_outs = [((c.shape[0], n_rest), F32, (c.shape[0], LANES), lambda i, j: (0, blk(i, j)))]

    def body(a, w, e, o):
        hv = a[0][...]
        gate = _dot(hv, w[0][...].astype(BF16))
        up = _dot(hv, w[1][...].astype(BF16))
        o[0][...] = (gate * jax.nn.sigmoid(gate) * up).astype(BF16)
        if ada is not None:
            o[1][...] = _dot(e[0][...].astype(BF16), e[1][...].astype(BF16)) + e[2][...]

    outs = _mm_call("ffn_in", m, f, bm, bn, [h], [(w_in, 0), (w_in, nj)], extras, [BF16], body, vmem_mib=56,
                    lhs_buffers=1, extra_outs=extra_outs, side_casts=side_casts)
    if ada is not None:
        outs = [outs[0]] + outs[2:] + [outs[1]]
    elif ada_separate is not None:
        outs = outs + [ada_separate]
    return outs


def _mm_res_call(name, a, w_bf, x, gate, coef, per, rows_per_group):
    m, k = a.shape
    n = w_bf.shape[1]
    bm = _pick(m, (1024, 512, 256, 128, 64, 32, 8) if k <= 4096 else (512, 256, 128, 64, 32, 8))
    bm = min(bm, rows_per_group)
    bn = _pick(n, (512, 256, 128))

    def body(lhs, rhs, e, o):
        acc = _dot(lhs[0][...], rhs[0][...])
        r = _row_periodic(acc, per, lambda a3: a3 * (coef * (1.0 + e[1][...])))
        o[0][...] = e[0][...] + r

    extras = [(x, (bm, bn), lambda i, j: (i, j)),
              (gate, (1, per, bn), lambda i, j: (i * bm // rows_per_group, 0, j))]
    (out,) = _mm_call(name, m, n, bm, bn, [a], [(w_bf, 0)], extras, [F32], body, vmem_mib=56)
    return out


def _head_rms(acc, gain, head_dim):
    outs = []
    for hh in range(acc.shape[1] // head_dim):
        y = acc[:, hh * head_dim:(hh + 1) * head_dim]
        ms = jnp.mean(y * y, axis=-1, keepdims=True)
        outs.append(y * lax.rsqrt(ms + EPS) * gain)
    return outs


def _log_sigmoid(z):
    return jnp.minimum(z, 0.0) - jnp.log1p(jnp.exp(-jnp.abs(z)))


def _proj_call(name, h, w_bf, col_off, n, mode, gain=None, head_dim=LANES, q_scale=1.0, w_f=None, b_f=None,
               side_casts=()):
    m, k = h.shape
    bm = _pick(m, (1024, 512, 256, 128, 64, 32, 8))
    bn = _pick(n, (512, 256, 128))
    extras, extra_outs = [], []
    if mode in ("q", "k"):
        extras.append((gain.reshape(1, head_dim), (1, head_dim), lambda i, j: (0, 0)))
    if mode == "v":
        extras.append((w_f, w_f.shape, lambda i, j: (0, 0)))
        extras.append((b_f, b_f.shape, lambda i, j: (0, 0)))
        extra_outs.append(((m, w_f.shape[1]), F32, (bm, w_f.shape[1]), lambda i, j: (i, 0)))
    out_dtypes = {"q": [BF16], "k": [F32, BF16], "v": [F32, BF16], "u": [F32]}[mode]

    def body(a, w, e, o):
        acc = _dot(a[0][...], w[0][...])
        if mode in ("q", "k"):
            parts = _head_rms(acc, e[0][...], head_dim)
            for hh, y in enumerate(parts):
                sl = slice(hh * head_dim, (hh + 1) * head_dim)
                if mode == "q":
                    o[0][:, sl] = (y * q_scale).astype(BF16)
                else:
                    o[0][:, sl] = y
                    o[1][:, sl] = y.astype(BF16)
        elif mode == "v":
            o[0][...] = acc
            o[1][...] = acc.astype(BF16)

            @pl.when(pl.program_id(1) == 0)
            def _():
                o[2][...] = _log_sigmoid(_dot(a[0][...], e[0][...]) + e[1][...])
        else:
            o[0][...] = acc

    assert col_off % bn == 0
    return _mm_call(name, m, n, bm, bn, [h], [(w_bf, col_off // bn)], extras, out_dtypes, body, vmem_mib=48,
                    extra_outs=extra_outs, side_casts=side_casts)


def _cumsum_call(lt):
    nb, nh, lp = lt.shape
    ck = CUMSUM_CHUNK
    assert lp % ck == 0
    rows = nb * nh
    rb = _pick(rows, (256, 128, 64, 32, 16, 8))

    def kern(x_ref, o_ref):
        row = lax.broadcasted_iota(jnp.int32, (ck, ck), 0)
        col = lax.broadcasted_iota(jnp.int32, (ck, ck), 1)
        upper = (row <= col).astype(F32)
        carry = jnp.zeros((rb, 1), F32)
        for c in range(lp // ck):
            xs = x_ref[:, c * ck:(c + 1) * ck]
            ys = jnp.dot(xs, upper, preferred_element_type=F32, precision=lax.Precision.HIGHEST) + carry
            o_ref[:, c * ck:(c + 1) * ck] = ys
            carry = ys[:, ck - 1:ck]

    out = pl.pallas_call(
        kern,
        out_shape=jax.ShapeDtypeStruct((rows, lp), F32),
        grid=(rows // rb,),
        in_specs=[pl.BlockSpec((rb, lp), lambda r: (r, 0))],
        out_specs=pl.BlockSpec((rb, lp), lambda r: (r, 0)),
        compiler_params=_params(1, 40),
        name="logf_cumsum",
    )(lt.reshape(rows, lp))
    return out.reshape(nb, nh, lp)


def _softmax_pv(parts, fq):
    m_keys = None
    for s, _ in parts:
        mx = jnp.max(s, axis=1, keepdims=True)
        m_keys = mx if m_keys is None else jnp.maximum(m_keys, mx)
    shift = fq - (m_keys + fq)
    l_sum, acc = None, None
    for s, v in parts:
        p = jnp.exp2(s + shift)
        ls = jnp.sum(p, axis=1, keepdims=True)
        pv = _dot(p.astype(BF16), v)
        l_sum = ls if l_sum is None else l_sum + ls
        acc = pv if acc is None else acc + pv
    return acc / l_sum


def _causal_mask(s):
    r = lax.broadcasted_iota(jnp.int32, s.shape, 0)
    c = lax.broadcasted_iota(jnp.int32, s.shape, 1)
    return jnp.where(c <= r, s, NEG_BIG)


def _attn_prompt_call(q, k, v, f_col, f_row, nb, seq, nh, head_dim):
    tq = _pick(seq, (256, 128, 64, 32, 8))
    nq = seq // tq
    hp = 2 if nh % 2 == 0 else 1
    hw = hp * head_dim

    def kern(q_ref, k_ref, v_ref, fc_ref, fr_ref, o_ref, fq_ref):
        fall = fc_ref[...]
        lane = lax.broadcasted_iota(jnp.int32, fall.shape, 1)
        for j in range(hp):
            hh = pl.program_id(1) * hp + j
            fq_ref[j] = jnp.sum(jnp.where(lane == hh, fall, 0.0), axis=1, keepdims=True) * LOG2E
        for qi in range(nq):
            lo, hi = qi * tq, (qi + 1) * tq
            for j in range(hp):
                cs = slice(j * head_dim, (j + 1) * head_dim)
                qv = q_ref[lo:hi, cs]
                s_d = _causal_mask(_dot_nt(qv, k_ref[lo:hi, cs]) - fr_ref[0, j, :, lo:hi] * LOG2E)
                parts = [(s_d, v_ref[lo:hi, cs])]
                if qi > 0:
                    s_o = _dot_nt(qv, k_ref[0:lo, cs]) - fr_ref[0, j, :, 0:lo] * LOG2E
                    parts.append((s_o, v_ref[0:lo, cs]))
                o_ref[lo:hi, cs] = _softmax_pv(parts, fq_ref[j, lo:hi, :]).astype(BF16)

    return pl.pallas_call(
        kern,
        out_shape=jax.ShapeDtypeStruct((nb * seq, nh * head_dim), BF16),
        grid=(nb, nh // hp),
        in_specs=[pl.BlockSpec((seq, hw), lambda b, h: (b, h)),
                  pl.BlockSpec((seq, hw), lambda b, h: (b, h)),
                  pl.BlockSpec((seq, hw), lambda b, h: (b, h)),
                  pl.BlockSpec((seq, nh), lambda b, h: (b, 0)),
                  pl.BlockSpec((1, hp, 1, seq), lambda b, h: (b, h, 0, 0))],
        out_specs=pl.BlockSpec((seq, hw), lambda b, h: (b, h)),
        scratch_shapes=[pltpu.VMEM((hp, seq, 1), F32)],
        compiler_params=_params(2, 40),
        name="attn_prompt",
    )(q, k, v, f_col, f_row)


def _attn_sample_call(q, k_new, v_new, cache_k, cache_v, fq_col, f_cache_row, f_new_row,
                      nb, s_len, past, nh, head_dim):
    n_steps = nb * nh
    n_slots = 3
    n_parts = 2 if past % (2 * SUBLANES) == 0 else 1
    part = past // n_parts

    def kern(q_ref, kn_ref, vn_ref, fq_ref, fc_ref, fn_ref, ck_hbm, cv_hbm, o_ref, kbuf, vbuf, sem):
        step = pl.program_id(0) * nh + pl.program_id(1)
        slot = lax.rem(step, n_slots)

        def copies(s_idx, slot_idx):
            bb = s_idx // nh
            hh = lax.rem(s_idx, nh)
            out = []
            for j in range(n_parts):
                rs = pl.ds(j * part, part)
                out.append(pltpu.make_async_copy(ck_hbm.at[bb, rs, hh, :], kbuf.at[slot_idx, rs],
                                                 sem.at[0, slot_idx, j]))
                out.append(pltpu.make_async_copy(cv_hbm.at[bb, rs, hh, :], vbuf.at[slot_idx, rs],
                                                 sem.at[1, slot_idx, j]))
            return out

        @pl.when(step == 0)
        def _():
            for ahead in range(min(n_slots - 1, n_steps)):
                for cp in copies(step + ahead, ahead):
                    cp.start()

        @pl.when(step + (n_slots - 1) < n_steps)
        def _():
            for cp in copies(step + (n_slots - 1), lax.rem(step + (n_slots - 1), n_slots)):
                cp.start()

        for cp in copies(step, slot):
            cp.wait()

        qv = q_ref[...]
        s_c = _dot_nt(qv, kbuf[slot].astype(BF16)) - fc_ref[0, 0] * LOG2E
        s_n = _causal_mask(_dot_nt(qv, kn_ref[...]) - fn_ref[0, 0] * LOG2E)
        parts = [(s_c, vbuf[slot].astype(BF16)), (s_n, vn_ref[...])]
        o_ref[...] = _softmax_pv(parts, fq_ref[0, 0] * LOG2E).astype(BF16)

    return pl.pallas_call(
        kern,
        out_shape=jax.ShapeDtypeStruct((nb * s_len, nh * head_dim), BF16),
        grid=(nb, nh),
        in_specs=[pl.BlockSpec((s_len, head_dim), lambda b, h: (b, h)),
                  pl.BlockSpec((s_len, head_dim), lambda b, h: (b, h)),
                  pl.BlockSpec((s_len, head_dim), lambda b, h: (b, h)),
                  pl.BlockSpec((1, 1, s_len, 1), lambda b, h: (b, h, 0, 0)),
                  pl.BlockSpec((1, 1, 1, past), lambda b, h: (b, h, 0, 0)),
                  pl.BlockSpec((1, 1, 1, s_len), lambda b, h: (b, h, 0, 0)),
                  pl.BlockSpec(memory_space=pl.ANY),
                  pl.BlockSpec(memory_space=pl.ANY)],
        out_specs=pl.BlockSpec((s_len, head_dim), lambda b, h: (b, h)),
        scratch_shapes=[pltpu.VMEM((n_slots, past, head_dim), F32),
                        pltpu.VMEM((n_slots, past, head_dim), F32),
                        pltpu.SemaphoreType.DMA((2, n_slots, n_parts))],
        compiler_params=_params(2, 40),
        name="attn_sample",
    )(q, k_new, v_new, fq_col, f_cache_row, f_new_row, cache_k, cache_v)


def _s5_call(name, u, wb, wc, a_pack, d_row, h0_re, h0_im, nb, t_len, batch_major=False):
    rows, d_ssm = u.shape
    n_gt = wb.shape[0]
    cw = wb.shape[1]
    sw = wb.shape[2] // 2
    nt = 2 if n_gt % 2 == 0 else 1
    tc = _pick(t_len, tuple(c for c in (64, 32, 16, 8, 4, 2, 1) if c * nb <= 1024))
    n_chunks = t_len // tc
    rc = tc * nb
    assert nb % SUBLANES == 0
    assert not batch_major or nb == SUBLANES
    n_steps = (n_gt // nt) * n_chunks
    uw = nt * cw

    def kern(u_ref, wb_ref, wc_ref, a_ref, d_ref, h0r_ref, h0i_ref, y_ref, hr_out, hi_out,
             hr_ref, hi_ref, bu_ref, *gather_scratch):
        c = pl.program_id(1)
        if batch_major:
            ubuf, sem = gather_scratch
            step = pl.program_id(0) * n_chunks + c
            slot = lax.rem(step, 2)

            def copies(s_idx, slot_idx):
                t0 = pl.multiple_of(lax.rem(s_idx, n_chunks) * tc, tc)
                c0 = pl.multiple_of((s_idx // n_chunks) * uw, uw)
                return [pltpu.make_async_copy(u_ref.at[b, pl.ds(t0, tc), pl.ds(c0, uw)],
                                              ubuf.at[slot_idx, :, b, :], sem.at[slot_idx, b])
                        for b in range(nb)]

            @pl.when(step == 0)
            def _():
                for cp in copies(step, slot):
                    cp.start()

            @pl.when(step + 1 < n_steps)
            def _():
                for cp in copies(step + 1, 1 - slot):
                    cp.start()

            for cp in copies(step, slot):
                cp.wait()
            u_all = ubuf[slot].reshape(rc, uw)
            u_tile = lambda g: u_all[:, g * cw:(g + 1) * cw]
        else:
            u_tile = lambda g: u_ref[:, g * cw:(g + 1) * cw]

        @pl.when(c == 0)
        def _():
            hr_ref[...] = h0r_ref[...]
            hi_ref[...] = h0i_ref[...]

        for g in range(nt):
            bu_ref[g] = _dot(u_tile(g).astype(BF16), wb_ref[g])
        for g in range(nt):
            ar = jnp.broadcast_to(a_ref[g, 0:1, :], (SUBLANES, sw))
            ai = jnp.broadcast_to(a_ref[g, 1:2, :], (SUBLANES, sw))
            for r in range(nb // SUBLANES):
                rs = slice(r * SUBLANES, (r + 1) * SUBLANES)
                hr = hr_ref[rs, g * sw:(g + 1) * sw]
                hi = hi_ref[rs, g * sw:(g + 1) * sw]
                for t in range(tc):
                    ts = slice(t * nb + r * SUBLANES, t * nb + (r + 1) * SUBLANES)
                    nhr = ar * hr - ai * hi + bu_ref[g, ts, 0:sw]
                    nhi = ar * hi + ai * hr + bu_ref[g, ts, sw:2 * sw]
                    bu_ref[g, ts, 0:sw] = nhr
                    bu_ref[g, ts, sw:2 * sw] = nhi
                    hr, hi = nhr, nhi
                hr_ref[rs, g * sw:(g + 1) * sw] = hr
                hi_ref[rs, g * sw:(g + 1) * sw] = hi
            cs = slice(g * cw, (g + 1) * cw)
            y = _dot(bu_ref[g].astype(BF16), wc_ref[g]) + d_ref[:, cs] * u_tile(g)
            y_ref[:, cs] = jax.nn.gelu(y, approximate=True).astype(BF16)

        @pl.when(c == n_chunks - 1)
        def _():
            hr_out[...] = hr_ref[...]
            hi_out[...] = hi_ref[...]

    return pl.pallas_call(
        kern,
        out_shape=[jax.ShapeDtypeStruct((rows, d_ssm), BF16),
                   jax.ShapeDtypeStruct((nb, n_gt * sw), F32),
                   jax.ShapeDtypeStruct((nb, n_gt * sw), F32)],
        grid=(n_gt // nt, n_chunks),
        in_specs=[pl.BlockSpec(memory_space=pl.ANY) if batch_major else
                  pl.BlockSpec((rc, nt * cw), lambda g, c: (c, g)),
                  pl.BlockSpec((nt, cw, 2 * sw), lambda g, c: (g, 0, 0)),
                  pl.BlockSpec((nt, 2 * sw, cw), lambda g, c: (g, 0, 0)),
                  pl.BlockSpec((nt, 2, sw), lambda g, c: (g, 0, 0)),
                  pl.BlockSpec((1, nt * cw), lambda g, c: (0, g)),
                  pl.BlockSpec((nb, nt * sw), lambda g, c: (0, g)),
                  pl.BlockSpec((nb, nt * sw), lambda g, c: (0, g))],
        out_specs=[pl.BlockSpec((rc, nt * cw), lambda g, c: (c, g)),
                   pl.BlockSpec((nb, nt * sw), lambda g, c: (0, g)),
                   pl.BlockSpec((nb, nt * sw), lambda g, c: (0, g))],
        scratch_shapes=[pltpu.VMEM((nb, nt * sw), F32),
                        pltpu.VMEM((nb, nt * sw), F32),
                        pltpu.VMEM((nt, rc, 2 * sw), F32)] + (
            [pltpu.VMEM((2, tc, nb, uw), F32), pltpu.SemaphoreType.DMA((2, nb))] if batch_major else []),
        compiler_params=_params(2, 40),
        name=name,
    )(u.reshape(nb, t_len, d_ssm) if batch_major else u, wb, wc, a_pack, d_row, h0_re, h0_im)


def _s5_discretize(a_re, a_im, log_dt, b_re, b_im, c_re, c_im):
    g, p = a_re.shape
    ch = b_re.shape[2]
    gpt = S5_GROUPS_PER_TILE
    n_gt = g // gpt
    dt = jnp.exp(log_dt.astype(F32))[:, None]
    mag = jnp.exp(a_re * dt)
    abar_re = mag * jnp.cos(a_im * dt)
    abar_im = mag * jnp.sin(a_im * dt)
    xr = abar_re - 1
    den = a_re * a_re + a_im * a_im
    coef_re = ((xr * a_re + abar_im * a_im) / den)[..., None]
    coef_im = ((abar_im * a_re - xr * a_im) / den)[..., None]
    bb_re = coef_re * b_re - coef_im * b_im
    bb_im = coef_re * b_im + coef_im * b_re
    eye = jnp.eye(gpt, dtype=F32)

    def blockdiag_in(bb):
        t = bb.reshape(n_gt, gpt, p, ch)
        return jnp.einsum("tgpc,gh->tgchp", t, eye).reshape(n_gt, gpt * ch, gpt * p)

    def blockdiag_out(cc):
        t = cc.reshape(n_gt, gpt, ch, p)
        return jnp.einsum("tgcp,gh->tgphc", t, eye).reshape(n_gt, gpt * p, gpt * ch)

    wb = jnp.concatenate([blockdiag_in(bb_re), blockdiag_in(bb_im)], axis=2).astype(BF16)
    wc = jnp.concatenate([blockdiag_out(c_re), -blockdiag_out(c_im)], axis=1).astype(BF16)
    a_pack = jnp.stack([abar_re.reshape(n_gt, gpt * p), abar_im.reshape(n_gt, gpt * p)], axis=1)
    return wb, wc, a_pack


def _merge_call(h, o_attn, y_ssm, w_gate_bf, b_gate, w_attn_out_bf, w_glu_bf):
    m, d = h.shape
    bm = _pick(m, (512, 256, 128, 64, 32, 8))
    bn = _pick(d, (512, 256, 128))
    nblk = d // bn

    def body(a, w, e, o):
        hv = a[0][...]
        ga = jax.nn.sigmoid(_dot(hv, w[0][...]) + e[0][...])
        gs = jax.nn.sigmoid(_dot(hv, w[1][...]) + e[1][...])
        ab = _dot(a[1][...], w[2][...])
        yv = a[2][...]
        sb = _dot(yv, w[3][...]) * jax.nn.sigmoid(_dot(yv, w[4][...]))
        o[0][...] = (ga * ab + gs * sb).astype(BF16)

    bg = b_gate.reshape(1, 2 * d)
    extras = [(bg, (1, bn), lambda i, j: (0, j)),
              (bg, (1, bn), lambda i, j: (0, j + nblk))]
    rhs = [(w_gate_bf, 0), (w_gate_bf, nblk), (w_attn_out_bf, 0), (w_glu_bf, 0), (w_glu_bf, nblk)]
    (merged,) = _mm_call("gated_merge", m, d, bm, bn, [h, o_attn, y_ssm], rhs, extras, [BF16], body,
                         vmem_mib=56)
    return merged


def _pad_time(x, lp):
    return jnp.pad(x, [(0, 0)] * (x.ndim - 1) + [(0, lp - x.shape[-1])])


def _layer(x, mod_rows, shared, dims, cache=None, ssm_h0=None):
    nb, t_len, d = x.shape
    nh, head_dim, n_groups, n_state = dims
    d_attn = nh * head_dim
    sample = cache is not None
    m = nb * t_len
    wts = shared
    raw = shared.pop("raw", None)
    ada = shared.pop("ada", None)
    if sample:
        xt = x.transpose(1, 0, 2).reshape(m, d)
        per, rpg = nb, m
        modg = lambda j: shared["mod"][mod_rows, j][None]
    else:
        xt = x.reshape(m, d)
        per, rpg = SUBLANES, t_len
        modg = lambda j: jnp.broadcast_to(shared["mod"][mod_rows, j][:, None, :], (nb, per, d))
    side = (lambda *names: [raw[k] for k in names]) if raw is not None else (lambda *names: [])

    h = _norm_mod_call(xt, wts["norm_g"][0], modg(1), modg(0), per, rpg)
    outs = _ffn_in_call(h, wts["ffn1_in"], side_casts=side("ffn1_out"), ada=ada)
    act = outs[0]
    if raw is not None:
        wts["ffn1_out"] = outs[1]
        w_in_bf = raw["w_in"].astype(BF16)
        wts["w_in"] = w_in_bf
        wts["w_f"] = jnp.pad(w_in_bf[:, 3 * d_attn:3 * d_attn + nh], ((0, 0), (0, LANES - nh)))
        wts["w_u"] = w_in_bf[:, 3 * d_attn + nh:]
    if ada is not None:
        n_all = shared["mod"].shape[0]
        shared["mod"] = jnp.concatenate([shared["mod"].reshape(n_all, -1), outs[-1]], axis=1).reshape(n_all, N_MOD, d)
    xt = _mm_res_call("ffn1_out", act, wts["ffn1_out"], xt, modg(2), HALF_STEP, per, rpg)

    h = _norm_mod_call(xt, wts["norm_g"][1], modg(4), modg(3), per, rpg)
    w_in_bf = wts["w_in"]
    outs = _proj_call("proj_q", h, w_in_bf, 0, d_attn, "q", gain=wts["q_norm_g"], head_dim=head_dim,
                      q_scale=head_dim ** -0.5 * LOG2E, side_casts=side("w_gate"))
    q = outs[0]
    if raw is not None:
        wts["w_gate"] = outs[1]
    outs = _proj_call("proj_k", h, w_in_bf, d_attn, d_attn, "k", gain=wts["k_norm_g"], head_dim=head_dim,
                      side_casts=side("w_glu"))
    k32, kbf = outs[0], outs[1]
    if raw is not None:
        wts["w_glu"] = outs[2]
    outs = _proj_call("proj_v", h, w_in_bf, 2 * d_attn, d_attn, "v", w_f=wts["w_f"], b_f=wts["b_f"],
                      side_casts=side("w_attn_out"))
    v32, vbf, lf = outs[0], outs[1], outs[2]
    if raw is not None:
        wts["w_attn_out"] = outs[3]
    outs = _proj_call("proj_u", h, wts["w_u"], 0, wts["w_u"].shape[1], "u", side_casts=side("w_out"))
    u = outs[0]
    if raw is not None:
        wts["w_out"] = outs[1]
    logf = lf[:, :nh]

    ck = CUMSUM_CHUNK
    if sample:
        cache_k, cache_v, cache_logf = cache
        past = cache_k.shape[1]
        logf_bt = logf.reshape(t_len, nb, nh).transpose(1, 0, 2)
        lt = jnp.concatenate([cache_logf.astype(F32), logf_bt], axis=1).transpose(0, 2, 1)
        lp = -(-(past + t_len) // ck) * ck
        f_row = _cumsum_call(_pad_time(lt, lp))
        f_cache_row = f_row[:, :, None, :past]
        f_new_row = f_row[:, :, None, past:past + t_len]
        fq_col = f_row[:, :, past:past + t_len, None]
        to_bt = lambda a: a.reshape(t_len, nb, -1).transpose(1, 0, 2).reshape(m, -1)
        o_bt = _attn_sample_call(to_bt(q), to_bt(kbf), to_bt(vbf), cache_k, cache_v,
                                 fq_col, f_cache_row, f_new_row, nb, t_len, past, nh, head_dim)
        o_attn = o_bt.reshape(nb, t_len, d_attn).transpose(1, 0, 2).reshape(m, d_attn)
        k_out = to_bt(k32).reshape(nb, t_len, nh, head_dim)
        v_out = to_bt(v32).reshape(nb, t_len, nh, head_dim)
        logf_out = logf_bt
        u_tm = u
        h0_re = ssm_h0[0].astype(F32).reshape(nb, n_groups * n_state)
        h0_im = ssm_h0[1].astype(F32).reshape(nb, n_groups * n_state)
    else:
        lt = logf.reshape(nb, t_len, nh).transpose(0, 2, 1)
        lp = -(-t_len // ck) * ck
        f_row = _cumsum_call(_pad_time(lt, lp))[:, :, :t_len]
        f_col = f_row.transpose(0, 2, 1).reshape(m, nh)
        o_attn = _attn_prompt_call(q, kbf, vbf, f_col, f_row[:, :, None, :], nb, t_len, nh, head_dim)
        k_out = k32.reshape(nb, t_len, nh, head_dim)
        v_out = v32.reshape(nb, t_len, nh, head_dim)
        logf_out = logf.reshape(nb, t_len, nh)
        gather = nb == SUBLANES
        u_tm = u if gather else u.reshape(nb, t_len, -1).transpose(1, 0, 2).reshape(m, -1)
        h0_re = jnp.zeros((nb, n_groups * n_state), F32)
        h0_im = h0_re

    y_tm, h_re, h_im = _s5_call("s5_sample" if sample else "s5_prompt", u_tm, wts["s5_wb"], wts["s5_wc"],
                                wts["s5_a"], wts["s5_d"], h0_re, h0_im, nb, t_len,
                                batch_major=not sample and gather)
    if sample:
        y_ssm = y_tm
    else:
        y_ssm = y_tm.reshape(t_len, nb, -1).transpose(1, 0, 2).reshape(m, -1)

    merged = _merge_call(h, o_attn, y_ssm, wts["w_gate"], wts["b_gate"], wts["w_attn_out"], wts["w_glu"])
    xt = _mm_res_call("mix_out", merged, wts["w_out"], xt, modg(5), 1.0, per, rpg)

    h = _norm_mod_call(xt, wts["norm_g"][2], modg(7), modg(6), per, rpg)
    outs = _ffn_in_call(h, wts["ffn2_in"], side_casts=side("ffn2_out"))
    act = outs[0]
    if raw is not None:
        wts["ffn2_out"] = outs[1]
    xt = _mm_res_call("ffn2_out", act, wts["ffn2_out"], xt, modg(8), HALF_STEP, per, rpg)

    if sample:
        y = xt.reshape(t_len, nb, d).transpose(1, 0, 2)
    else:
        y = xt.reshape(nb, t_len, d)
    return (y, k_out, v_out, logf_out,
            h_re.reshape(nb, n_groups, n_state), h_im.reshape(nb, n_groups, n_state))


def kernel(x_prompt, x_sample, cache_k, cache_v, cache_logf, state_ssm_re, state_ssm_im, c_prompt, c_sample, w_ada, b_ada, norm_g, w_ffn1_in, w_ffn1_out, w_in, b_forget, q_norm_g, k_norm_g, w_attn_out, ssm_a_re, ssm_a_im, ssm_log_dt, ssm_b_re, ssm_b_im, ssm_c_re, ssm_c_im, ssm_d, w_glu, w_gate, b_gate, w_out, w_ffn2_in, w_ffn2_out):
    depth = w_ada.shape[0]
    nbp = x_prompt.shape[0]
    d = x_prompt.shape[2]
    nh, head_dim = cache_k.shape[3], cache_k.shape[4]
    n_groups, n_state, n_ch = ssm_b_re.shape[1:]
    d_attn = nh * head_dim
    d_ssm = n_groups * n_ch
    dims = (nh, head_dim, n_groups, n_state)
    assert head_dim == LANES and nh <= LANES
    assert S5_GROUPS_PER_TILE * n_ch == LANES and n_groups % S5_GROUPS_PER_TILE == 0

    xp, xs = x_prompt, x_sample
    outs_p, outs_s = [], []
    for l in range(depth):
        c_all = jnp.concatenate([c_prompt, c_sample], axis=0)
        b_ada_row = b_ada[l].reshape(1, -1)
        n_first = 2 * d
        mod_first = _ada_call(c_all, w_ada[l], b_ada_row, 0, n_first).reshape(c_all.shape[0], 2, d)
        wb, wc, a_pack = _s5_discretize(ssm_a_re[l].astype(F32), ssm_a_im[l].astype(F32), ssm_log_dt[l],
                                        ssm_b_re[l].astype(F32), ssm_b_im[l].astype(F32),
                                        ssm_c_re[l].astype(F32), ssm_c_im[l].astype(F32))
        shared = {
            "mod": mod_first,
            "ada": (c_all, w_ada[l], b_ada_row, n_first),
            "raw": {"ffn1_out": w_ffn1_out[l], "ffn2_out": w_ffn2_out[l], "w_in": w_in[l],
                    "w_gate": w_gate[l], "w_glu": w_glu[l], "w_attn_out": w_attn_out[l], "w_out": w_out[l]},
            "norm_g": norm_g[l].astype(F32),
            "ffn1_in": w_ffn1_in[l], "ffn2_in": w_ffn2_in[l],
            "b_f": jnp.pad(b_forget[l].astype(F32), (0, LANES - nh)).reshape(1, LANES),
            "q_norm_g": q_norm_g[l].astype(F32), "k_norm_g": k_norm_g[l].astype(F32),
            "b_gate": b_gate[l].astype(F32),
            "s5_wb": wb, "s5_wc": wc, "s5_a": a_pack, "s5_d": ssm_d[l].astype(F32).reshape(1, d_ssm),
        }
        xp, k1, v1, f1, r1, i1 = _layer(xp, slice(0, nbp), shared, dims)
        xs, k2, v2, f2, r2, i2 = _layer(xs, slice(nbp, None), shared, dims,
                                        cache=(cache_k[l], cache_v[l], cache_logf[l]),
                                        ssm_h0=(state_ssm_re[l], state_ssm_im[l]))
        outs_p.append((k1, v1, f1, r1, i1))
        outs_s.append((k2, v2, f2, r2, i2))

    stack = lambda outs, idx: jnp.stack([o[idx] for o in outs])
    return (xp, xs,
            stack(outs_p, 0), stack(outs_p, 1), stack(outs_p, 2), stack(outs_p, 3), stack(outs_p, 4),
            stack(outs_s, 0), stack(outs_s, 1), stack(outs_s, 2), stack(outs_s, 3), stack(outs_s, 4))
```

```python
import functools
import math

import jax
import jax.numpy as jnp
from jax import lax
from jax.experimental import pallas as pl
from jax.experimental.pallas import tpu as pltpu

F32 = jnp.float32
BF16 = jnp.bfloat16

EPS = 1e-6
HALF_STEP = 0.5
N_MOD = 9
NEG_BIG = -1e30
LOG2E = math.log2(math.e)
LANES = 128
SUBLANES = 8
MIB = 1024 * 1024
CUMSUM_CHUNK = 256
S5_GROUPS_PER_TILE = 8


def _params(n_axes, vmem_mib):
    return pltpu.CompilerParams(
        dimension_semantics=("arbitrary",) * n_axes,
        vmem_limit_bytes=vmem_mib * MIB,
    )


def _dot(a, b):
    return jnp.dot(a, b, preferred_element_type=F32)


def _dot_nt(a, b):
    return lax.dot_general(a, b, (((1,), (1,)), ((), ())), preferred_element_type=F32)


def _row_periodic(acc, per, fn):
    m, n = acc.shape
    return fn(acc.reshape(m // per, per, n)).reshape(m, n)


def _cast_rows(r, steps):
    tile = 2 * SUBLANES
    for rows in range(tile, r + 1, tile):
        if r % rows == 0 and r // rows <= steps:
            return rows
    return None


def _mm_call(name, m, n, bm, bn, lhs, rhs, extras, out_dtypes, body, vmem_mib=48, lhs_buffers=2,
             extra_outs=(), side_casts=()):
    assert m % bm == 0 and n % bn == 0, (name, m, n, bm, bn)
    nj = n // bn
    steps = (m // bm) * nj
    in_specs = []
    for a in lhs:
        mode = {} if lhs_buffers == 2 else {"pipeline_mode": pl.Buffered(lhs_buffers)}
        in_specs.append(pl.BlockSpec((bm, a.shape[1]), lambda i, j: (i, 0), **mode))
    for w, off in rhs:
        in_specs.append(pl.BlockSpec((w.shape[0], bn), lambda i, j, off=off: (0, j + off)))
    for _, bs, im in extras:
        in_specs.append(pl.BlockSpec(bs, im))
    out_specs = [pl.BlockSpec((bm, bn), lambda i, j: (i, j)) for _ in out_dtypes]
    out_shape = [jax.ShapeDtypeStruct((m, n), dt) for dt in out_dtypes]
    for shape, dt, bs, im in extra_outs:
        out_specs.append(pl.BlockSpec(bs, im))
        out_shape.append(jax.ShapeDtypeStruct(shape, dt))
    hosted = [arr for arr in side_casts if _cast_rows(arr.shape[0], steps) is not None]
    for arr in hosted:
        r, c = arr.shape
        rows = _cast_rows(r, steps)
        im = lambda i, j, last=r // rows - 1: (jnp.minimum(i * nj + j, last), 0)
        in_specs.append(pl.BlockSpec((rows, c), im))
        out_specs.append(pl.BlockSpec((rows, c), im))
        out_shape.append(jax.ShapeDtypeStruct((r, c), BF16))
    nl, nr, ne, ns = len(lhs), len(rhs), len(extras), len(hosted)
    n_in = nl + nr + ne + ns
    n_out = len(out_dtypes) + len(extra_outs)

    def kern(*refs):
        body(refs[:nl], refs[nl:nl + nr], refs[nl + nr:nl + nr + ne], refs[n_in:n_in + n_out])
        for s in range(ns):
            refs[n_in + n_out + s][...] = refs[nl + nr + ne + s][...].astype(BF16)

    outs = pl.pallas_call(
        kern,
        out_shape=out_shape,
        grid=(m // bm, nj),
        in_specs=in_specs,
        out_specs=out_specs,
        compiler_params=_params(2, vmem_mib),
        name=name,
    )(*lhs, *[w for w, _ in rhs], *[e for e, _, _ in extras], *hosted)
    outs = list(outs)
    main, cast_done = outs[:n_out], iter(outs[n_out:])
    casts = [next(cast_done) if any(arr is h_ for h_ in hosted) else arr.astype(BF16) for arr in side_casts]
    return main + casts


def _pick(total, prefs):
    for p in prefs:
        if total % p == 0:
            return p
    return total


def _ada_call(c, w_ada, b_ada, col0, n_cols):
    mb, d = c.shape
    bn = math.gcd(_pick(n_cols, (512, 256, 128)), col0) if col0 else _pick(n_cols, (512, 256, 128))
    assert bn % LANES == 0 and n_cols % bn == 0
    off = col0 // bn

    def kern(c_ref, w_ref, b_ref, o_ref):
        o_ref[...] = _dot(c_ref[...].astype(BF16), w_ref[...].astype(BF16)) + b_ref[...]

    return pl.pallas_call(
        kern,
        out_shape=jax.ShapeDtypeStruct((mb, n_cols), F32),
        grid=(n_cols // bn,),
        in_specs=[pl.BlockSpec((mb, d), lambda j: (0, 0)),
                  pl.BlockSpec((d, bn), lambda j: (0, j + off)),
                  pl.BlockSpec((1, bn), lambda j: (0, j + off))],
        out_specs=pl.BlockSpec((mb, bn), lambda j: (0, j)),
        compiler_params=_params(1, 40),
        name="ada_mod",
    )(c, w_ada, b_ada)


def _norm_mod_call(x, g, scale, shift, per, rows_per_group):
    m, d = x.shape
    bm = _pick(m, (512, 256, 128, 64, 32, 8))
    bm = min(bm, rows_per_group)

    rows = SUBLANES
    assert per % rows == 0 and bm % per == 0

    def kern(x_ref, g_ref, sc_ref, sh_ref, o_ref):
        for c in range(bm // rows):
            rs = slice(c * rows, (c + 1) * rows)
            ps = slice((c * rows) % per, (c * rows) % per + rows)
            xv = x_ref[rs, :]
            ms = jnp.mean(xv * xv, axis=-1, keepdims=True)
            y = xv * lax.rsqrt(ms + EPS) * g_ref[...]
            y = y * (1.0 + sc_ref[0, ps, :]) + sh_ref[0, ps, :]
            o_ref[rs, :] = y.astype(BF16)

    grp = lambda i: (i * bm // rows_per_group, 0, 0)
    return pl.pallas_call(
        kern,
        out_shape=jax.ShapeDtypeStruct((m, d), BF16),
        grid=(m // bm,),
        in_specs=[pl.BlockSpec((bm, d), lambda i: (i, 0)),
                  pl.BlockSpec((1, d), lambda i: (0, 0)),
                  pl.BlockSpec((1, per, d), grp),
                  pl.BlockSpec((1, per, d), grp)],
        out_specs=pl.BlockSpec((bm, d), lambda i: (i, 0)),
        compiler_params=_params(1, 40),
        name="norm_mod",
    )(x, g.reshape(1, d), scale, shift)


def _ffn_in_call(h, w_in, side_casts=(), ada=None):
    m, d = h.shape
    f = w_in.shape[1] // 2
    bm = _pick(m, (2048, 1024, 512, 256, 128, 64, 32, 8))
    bn = _pick(f, (256, 128))
    nj = f // bn
    steps = (m // bm) * nj
    extras, extra_outs = [], []
    ada_separate = None
    if ada is not None:
        c, w_ada, b_ada, col0 = ada
        n_rest = w_ada.shape[1] - col0
        assert col0 % LANES == 0 and n_rest % LANES == 0
        if n_rest // LANES > steps:
            ada_separate, ada = _ada_call(c, w_ada, b_ada, col0, n_rest), None
    if ada is not None:
        blk = lambda i, j: jnp.minimum(i * nj + j, n_rest // LANES - 1)
        extras = [(c, c.shape, lambda i, j: (0, 0)),
                  (w_ada, (w_ada.shape[0], LANES), lambda i, j: (0, col0 // LANES + blk(i, j))),
                  (b_ada, (1, LANES), lambda i, j: (0, col0 // LANES + blk(i, j)))]
        extra_outs = [((c.shape[0], n_rest), F32, (c.shape[0], LANES), lambda i, j: (0, blk(i, j)))]

    def body(a, w, e, o):
        hv = a[0][...]
        gate = _dot(hv, w[0][...].astype(BF16))
        up = _dot(hv, w[1][...].astype(BF16))
        o[0][...] = (gate * jax.nn.sigmoid(gate) * up).astype(BF16)
        if ada is not None:
            o[1][...] = _dot(e[0][...].astype(BF16), e[1][...].astype(BF16)) + e[2][...]

    outs = _mm_call("ffn_in", m, f, bm, bn, [h], [(w_in, 0), (w_in, nj)], extras, [BF16], body, vmem_mib=56,
                    lhs_buffers=1, extra_outs=extra_outs, side_casts=side_casts)
    if ada is not None:
        outs = [outs[0]] + outs[2:] + [outs[1]]
    elif ada_separate is not None:
        outs = outs + [ada_separate]
    return outs


def _mm_res_call(name, a, w_bf, x, gate, coef, per, rows_per_group):
    m, k = a.shape
    n = w_bf.shape[1]
    bm = _pick(m, (1024, 512, 256, 128, 64, 32, 8) if k <= 4096 else (512, 256, 128, 64, 32, 8))
    bm = min(bm, rows_per_group)
    bn = _pick(n, (1024, 512, 256, 128) if k <= 4096 else (512, 256, 128))

    def body(lhs, rhs, e, o):
        acc = _dot(lhs[0][...], rhs[0][...])
        r = _row_periodic(acc, per, lambda a3: a3 * (coef * (1.0 + e[1][...])))
        o[0][...] = e[0][...] + r

    extras = [(x, (bm, bn), lambda i, j: (i, j)),
              (gate, (1, per, bn), lambda i, j: (i * bm // rows_per_group, 0, j))]
    (out,) = _mm_call(name, m, n, bm, bn, [a], [(w_bf, 0)], extras, [F32], body, vmem_mib=56)
    return out


def _head_rms(acc, gain, head_dim):
    outs = []
    for hh in range(acc.shape[1] // head_dim):
        y = acc[:, hh * head_dim:(hh + 1) * head_dim]
        ms = jnp.mean(y * y, axis=-1, keepdims=True)
        outs.append(y * lax.rsqrt(ms + EPS) * gain)
    return outs


def _log_sigmoid(z):
    return jnp.minimum(z, 0.0) - jnp.log1p(jnp.exp(-jnp.abs(z)))


def _proj_call(name, h, w_bf, col_off, n, mode, gain=None, head_dim=LANES, q_scale=1.0, w_f=None, b_f=None,
               side_casts=()):
    m, k = h.shape
    bm = _pick(m, (1024, 512, 256, 128, 64, 32, 8))
    bn = _pick(n, (1024, 512, 256, 128) if mode in ("q", "u") else (512, 256, 128))
    extras, extra_outs = [], []
    if mode in ("q", "k"):
        extras.append((gain.reshape(1, head_dim), (1, head_dim), lambda i, j: (0, 0)))
    if mode == "v":
        extras.append((w_f, w_f.shape, lambda i, j: (0, 0)))
        extras.append((b_f, b_f.shape, lambda i, j: (0, 0)))
        extra_outs.append(((m, w_f.shape[1]), F32, (bm, w_f.shape[1]), lambda i, j: (i, 0)))
    out_dtypes = {"q": [BF16], "k": [F32, BF16], "v": [F32, BF16], "u": [F32]}[mode]

    def body(a, w, e, o):
        acc = _dot(a[0][...], w[0][...])
        if mode in ("q", "k"):
            parts = _head_rms(acc, e[0][...], head_dim)
            for hh, y in enumerate(parts):
                sl = slice(hh * head_dim, (hh + 1) * head_dim)
                if mode == "q":
                    o[0][:, sl] = (y * q_scale).astype(BF16)
                else:
                    o[0][:, sl] = y
                    o[1][:, sl] = y.astype(BF16)
        elif mode == "v":
            o[0][...] = acc
            o[1][...] = acc.astype(BF16)

            @pl.when(pl.program_id(1) == 0)
            def _():
                o[2][...] = _log_sigmoid(_dot(a[0][...], e[0][...]) + e[1][...])
        else:
            o[0][...] = acc

    assert col_off % bn == 0
    return _mm_call(name, m, n, bm, bn, [h], [(w_bf, col_off // bn)], extras, out_dtypes, body, vmem_mib=56,
                    extra_outs=extra_outs, side_casts=side_casts)


def _cumsum_call(lt):
    nb, nh, lp = lt.shape
    ck = CUMSUM_CHUNK
    assert lp % ck == 0
    rows = nb * nh
    rb = _pick(rows, (256, 128, 64, 32, 16, 8))

    def kern(x_ref, o_ref):
        row = lax.broadcasted_iota(jnp.int32, (ck, ck), 0)
        col = lax.broadcasted_iota(jnp.int32, (ck, ck), 1)
        upper = (row <= col).astype(F32)
        carry = jnp.zeros((rb, 1), F32)
        for c in range(lp // ck):
            xs = x_ref[:, c * ck:(c + 1) * ck]
            ys = jnp.dot(xs, upper, preferred_element_type=F32, precision=lax.Precision.HIGHEST) + carry
            o_ref[:, c * ck:(c + 1) * ck] = ys
            carry = ys[:, ck - 1:ck]

    out = pl.pallas_call(
        kern,
        out_shape=jax.ShapeDtypeStruct((rows, lp), F32),
        grid=(rows // rb,),
        in_specs=[pl.BlockSpec((rb, lp), lambda r: (r, 0))],
        out_specs=pl.BlockSpec((rb, lp), lambda r: (r, 0)),
        compiler_params=_params(1, 40),
        name="logf_cumsum",
    )(lt.reshape(rows, lp))
    return out.reshape(nb, nh, lp)


def _softmax_pv(parts, fq):
    m_keys = None
    for s, _ in parts:
        mx = jnp.max(s, axis=1, keepdims=True)
        m_keys = mx if m_keys is None else jnp.maximum(m_keys, mx)
    shift = fq - (m_keys + fq)
    l_sum, acc = None, None
    for s, v in parts:
        p = jnp.exp2(s + shift)
        ls = jnp.sum(p, axis=1, keepdims=True)
        pv = _dot(p.astype(BF16), v)
        l_sum = ls if l_sum is None else l_sum + ls
        acc = pv if acc is None else acc + pv
    return acc / l_sum


def _causal_mask(s):
    r = lax.broadcasted_iota(jnp.int32, s.shape, 0)
    c = lax.broadcasted_iota(jnp.int32, s.shape, 1)
    return jnp.where(c <= r, s, NEG_BIG)


def _attn_prompt_call(q, k, v, f_col, f_row, nb, seq, nh, head_dim):
    tq = _pick(seq, (256, 128, 64, 32, 8))
    nq = seq // tq
    hp = 2 if nh % 2 == 0 else 1
    hw = hp * head_dim

    def kern(q_ref, k_ref, v_ref, fc_ref, fr_ref, o_ref, fq_ref):
        fall = fc_ref[...]
        lane = lax.broadcasted_iota(jnp.int32, fall.shape, 1)
        for j in range(hp):
            hh = pl.program_id(1) * hp + j
            fq_ref[j] = jnp.sum(jnp.where(lane == hh, fall, 0.0), axis=1, keepdims=True) * LOG2E
        for qi in range(nq):
            lo, hi = qi * tq, (qi + 1) * tq
            for j in range(hp):
                cs = slice(j * head_dim, (j + 1) * head_dim)
                qv = q_ref[lo:hi, cs]
                s_d = _causal_mask(_dot_nt(qv, k_ref[lo:hi, cs]) - fr_ref[0, j, :, lo:hi] * LOG2E)
                parts = [(s_d, v_ref[lo:hi, cs])]
                if qi > 0:
                    s_o = _dot_nt(qv, k_ref[0:lo, cs]) - fr_ref[0, j, :, 0:lo] * LOG2E
                    parts.append((s_o, v_ref[0:lo, cs]))
                o_ref[lo:hi, cs] = _softmax_pv(parts, fq_ref[j, lo:hi, :]).astype(BF16)

    return pl.pallas_call(
        kern,
        out_shape=jax.ShapeDtypeStruct((nb * seq, nh * head_dim), BF16),
        grid=(nb, nh // hp),
        in_specs=[pl.BlockSpec((seq, hw), lambda b, h: (b, h)),
                  pl.BlockSpec((seq, hw), lambda b, h: (b, h)),
                  pl.BlockSpec((seq, hw), lambda b, h: (b, h)),
                  pl.BlockSpec((seq, nh), lambda b, h: (b, 0)),
                  pl.BlockSpec((1, hp, 1, seq), lambda b, h: (b, h, 0, 0))],
        out_specs=pl.BlockSpec((seq, hw), lambda b, h: (b, h)),
        scratch_shapes=[pltpu.VMEM((hp, seq, 1), F32)],
        compiler_params=_params(2, 40),
        name="attn_prompt",
    )(q, k, v, f_col, f_row)


def _attn_sample_call(q, k_new, v_new, cache_k, cache_v, fq_col, f_cache_row, f_new_row,
                      nb, s_len, past, nh, head_dim):
    n_steps = nb * nh
    n_slots = 3
    n_parts = 2 if past % (2 * SUBLANES) == 0 else 1
    part = past // n_parts

    def kern(q_ref, kn_ref, vn_ref, fq_ref, fc_ref, fn_ref, ck_hbm, cv_hbm, o_ref, kbuf, vbuf, sem):
        step = pl.program_id(0) * nh + pl.program_id(1)
        slot = lax.rem(step, n_slots)

        def copies(s_idx, slot_idx):
            bb = s_idx // nh
            hh = lax.rem(s_idx, nh)
            out = []
            for j in range(n_parts):
                rs = pl.ds(j * part, part)
                out.append(pltpu.make_async_copy(ck_hbm.at[bb, rs, hh, :], kbuf.at[slot_idx, rs],
                                                 sem.at[0, slot_idx, j]))
                out.append(pltpu.make_async_copy(cv_hbm.at[bb, rs, hh, :], vbuf.at[slot_idx, rs],
                                                 sem.at[1, slot_idx, j]))
            return out

        @pl.when(step == 0)
        def _():
            for ahead in range(min(n_slots - 1, n_steps)):
                for cp in copies(step + ahead, ahead):
                    cp.start()

        @pl.when(step + (n_slots - 1) < n_steps)
        def _():
            for cp in copies(step + (n_slots - 1), lax.rem(step + (n_slots - 1), n_slots)):
                cp.start()

        for cp in copies(step, slot):
            cp.wait()

        qv = q_ref[...]
        s_c = _dot_nt(qv, kbuf[slot].astype(BF16)) - fc_ref[0, 0] * LOG2E
        s_n = _causal_mask(_dot_nt(qv, kn_ref[...]) - fn_ref[0, 0] * LOG2E)
        parts = [(s_c, vbuf[slot].astype(BF16)), (s_n, vn_ref[...])]
        o_ref[...] = _softmax_pv(parts, fq_ref[0, 0] * LOG2E).astype(BF16)

    return pl.pallas_call(
        kern,
        out_shape=jax.ShapeDtypeStruct((nb * s_len, nh * head_dim), BF16),
        grid=(nb, nh),
        in_specs=[pl.BlockSpec((s_len, head_dim), lambda b, h: (b, h)),
                  pl.BlockSpec((s_len, head_dim), lambda b, h: (b, h)),
                  pl.BlockSpec((s_len, head_dim), lambda b, h: (b, h)),
                  pl.BlockSpec((1, 1, s_len, 1), lambda b, h: (b, h, 0, 0)),
                  pl.BlockSpec((1, 1, 1, past), lambda b, h: (b, h, 0, 0)),
                  pl.BlockSpec((1, 1, 1, s_len), lambda b, h: (b, h, 0, 0)),
                  pl.BlockSpec(memory_space=pl.ANY),
                  pl.BlockSpec(memory_space=pl.ANY)],
        out_specs=pl.BlockSpec((s_len, head_dim), lambda b, h: (b, h)),
        scratch_shapes=[pltpu.VMEM((n_slots, past, head_dim), F32),
                        pltpu.VMEM((n_slots, past, head_dim), F32),
                        pltpu.SemaphoreType.DMA((2, n_slots, n_parts))],
        compiler_params=_params(2, 40),
        name="attn_sample",
    )(q, k_new, v_new, fq_col, f_cache_row, f_new_row, cache_k, cache_v)


def _s5_call(name, u, wb, wc, a_pack, d_row, h0_re, h0_im, nb, t_len, batch_major=False):
    rows, d_ssm = u.shape
    n_gt = wb.shape[0]
    cw = wb.shape[1]
    sw = wb.shape[2] // 2
    nt = 2 if n_gt % 2 == 0 else 1
    tc = _pick(t_len, tuple(c for c in (128, 64, 32, 16, 8, 4, 2, 1) if c * nb <= 1024))
    n_chunks = t_len // tc
    rc = tc * nb
    assert nb % SUBLANES == 0
    assert not batch_major or nb == SUBLANES
    n_steps = (n_gt // nt) * n_chunks
    uw = nt * cw

    def kern(u_ref, wb_ref, wc_ref, a_ref, d_ref, h0r_ref, h0i_ref, y_ref, hr_out, hi_out,
             hr_ref, hi_ref, bu_ref, *gather_scratch):
        c = pl.program_id(1)
        if batch_major:
            ubuf, sem = gather_scratch
            step = pl.program_id(0) * n_chunks + c
            slot = lax.rem(step, 2)

            def copies(s_idx, slot_idx):
                t0 = pl.multiple_of(lax.rem(s_idx, n_chunks) * tc, tc)
                c0 = pl.multiple_of((s_idx // n_chunks) * uw, uw)
                return [pltpu.make_async_copy(u_ref.at[b, pl.ds(t0, tc), pl.ds(c0, uw)],
                                              ubuf.at[slot_idx, :, b, :], sem.at[slot_idx, b])
                        for b in range(nb)]

            @pl.when(step == 0)
            def _():
                for cp in copies(step, slot):
                    cp.start()

            @pl.when(step + 1 < n_steps)
            def _():
                for cp in copies(step + 1, 1 - slot):
                    cp.start()

            for cp in copies(step, slot):
                cp.wait()
            u_all = ubuf[slot].reshape(rc, uw)
            u_tile = lambda g: u_all[:, g * cw:(g + 1) * cw]
        else:
            u_tile = lambda g: u_ref[:, g * cw:(g + 1) * cw]

        @pl.when(c == 0)
        def _():
            hr_ref[...] = h0r_ref[...]
            hi_ref[...] = h0i_ref[...]

        for g in range(nt):
            bu_ref[g] = _dot(u_tile(g).astype(BF16), wb_ref[g])
        for g in range(nt):
            ar = jnp.broadcast_to(a_ref[g, 0:1, :], (SUBLANES, sw))
            ai = jnp.broadcast_to(a_ref[g, 1:2, :], (SUBLANES, sw))
            for r in range(nb // SUBLANES):
                rs = slice(r * SUBLANES, (r + 1) * SUBLANES)
                hr = hr_ref[rs, g * sw:(g + 1) * sw]
                hi = hi_ref[rs, g * sw:(g + 1) * sw]
                for t in range(tc):
                    ts = slice(t * nb + r * SUBLANES, t * nb + (r + 1) * SUBLANES)
                    nhr = ar * hr - ai * hi + bu_ref[g, ts, 0:sw]
                    nhi = ar * hi + ai * hr + bu_ref[g, ts, sw:2 * sw]
                    bu_ref[g, ts, 0:sw] = nhr
                    bu_ref[g, ts, sw:2 * sw] = nhi
                    hr, hi = nhr, nhi
                hr_ref[rs, g * sw:(g + 1) * sw] = hr
                hi_ref[rs, g * sw:(g + 1) * sw] = hi
            cs = slice(g * cw, (g + 1) * cw)
            y = _dot(bu_ref[g].astype(BF16), wc_ref[g]) + d_ref[:, cs] * u_tile(g)
            y_ref[:, cs] = jax.nn.gelu(y, approximate=True).astype(BF16)

        @pl.when(c == n_chunks - 1)
        def _():
            hr_out[...] = hr_ref[...]
            hi_out[...] = hi_ref[...]

    return pl.pallas_call(
        kern,
        out_shape=[jax.ShapeDtypeStruct((rows, d_ssm), BF16),
                   jax.ShapeDtypeStruct((nb, n_gt * sw), F32),
                   jax.ShapeDtypeStruct((nb, n_gt * sw), F32)],
        grid=(n_gt // nt, n_chunks),
        in_specs=[pl.BlockSpec(memory_space=pl.ANY) if batch_major else
                  pl.BlockSpec((rc, nt * cw), lambda g, c: (c, g)),
                  pl.BlockSpec((nt, cw, 2 * sw), lambda g, c: (g, 0, 0)),
                  pl.BlockSpec((nt, 2 * sw, cw), lambda g, c: (g, 0, 0)),
                  pl.BlockSpec((nt, 2, sw), lambda g, c: (g, 0, 0)),
                  pl.BlockSpec((1, nt * cw), lambda g, c: (0, g)),
                  pl.BlockSpec((nb, nt * sw), lambda g, c: (0, g)),
                  pl.BlockSpec((nb, nt * sw), lambda g, c: (0, g))],
        out_specs=[pl.BlockSpec((rc, nt * cw), lambda g, c: (c, g)),
                   pl.BlockSpec((nb, nt * sw), lambda g, c: (0, g)),
                   pl.BlockSpec((nb, nt * sw), lambda g, c: (0, g))],
        scratch_shapes=[pltpu.VMEM((nb, nt * sw), F32),
                        pltpu.VMEM((nb, nt * sw), F32),
                        pltpu.VMEM((nt, rc, 2 * sw), F32)] + (
            [pltpu.VMEM((2, tc, nb, uw), F32), pltpu.SemaphoreType.DMA((2, nb))] if batch_major else []),
        compiler_params=_params(2, 40),
        name=name,
    )(u.reshape(nb, t_len, d_ssm) if batch_major else u, wb, wc, a_pack, d_row, h0_re, h0_im)


def _s5_discretize(a_re, a_im, log_dt, b_re, b_im, c_re, c_im):
    g, p = a_re.shape
    ch = b_re.shape[2]
    gpt = S5_GROUPS_PER_TILE
    n_gt = g // gpt
    dt = jnp.exp(log_dt.astype(F32))[:, None]
    mag = jnp.exp(a_re * dt)
    abar_re = mag * jnp.cos(a_im * dt)
    abar_im = mag * jnp.sin(a_im * dt)
    xr = abar_re - 1
    den = a_re * a_re + a_im * a_im
    coef_re = ((xr * a_re + abar_im * a_im) / den)[..., None]
    coef_im = ((abar_im * a_re - xr * a_im) / den)[..., None]
    bb_re = coef_re * b_re - coef_im * b_im
    bb_im = coef_re * b_im + coef_im * b_re
    eye = jnp.eye(gpt, dtype=F32)

    def blockdiag_in(bb):
        t = bb.reshape(n_gt, gpt, p, ch)
        return jnp.einsum("tgpc,gh->tgchp", t, eye).reshape(n_gt, gpt * ch, gpt * p)

    def blockdiag_out(cc):
        t = cc.reshape(n_gt, gpt, ch, p)
        return jnp.einsum("tgcp,gh->tgphc", t, eye).reshape(n_gt, gpt * p, gpt * ch)

    wb = jnp.concatenate([blockdiag_in(bb_re), blockdiag_in(bb_im)], axis=2).astype(BF16)
    wc = jnp.concatenate([blockdiag_out(c_re), -blockdiag_out(c_im)], axis=1).astype(BF16)
    a_pack = jnp.stack([abar_re.reshape(n_gt, gpt * p), abar_im.reshape(n_gt, gpt * p)], axis=1)
    return wb, wc, a_pack


def _merge_call(h, o_attn, y_ssm, w_gate_bf, b_gate, w_attn_out_bf, w_glu_bf):
    m, d = h.shape
    bm = _pick(m, (512, 256, 128, 64, 32, 8))
    bn = _pick(d, (512, 256, 128))
    nblk = d // bn

    def body(a, w, e, o):
        hv = a[0][...]
        ga = jax.nn.sigmoid(_dot(hv, w[0][...]) + e[0][...])
        gs = jax.nn.sigmoid(_dot(hv, w[1][...]) + e[1][...])
        ab = _dot(a[1][...], w[2][...])
        yv = a[2][...]
        sb = _dot(yv, w[3][...]) * jax.nn.sigmoid(_dot(yv, w[4][...]))
        o[0][...] = (ga * ab + gs * sb).astype(BF16)

    bg = b_gate.reshape(1, 2 * d)
    extras = [(bg, (1, bn), lambda i, j: (0, j)),
              (bg, (1, bn), lambda i, j: (0, j + nblk))]
    rhs = [(w_gate_bf, 0), (w_gate_bf, nblk), (w_attn_out_bf, 0), (w_glu_bf, 0), (w_glu_bf, nblk)]
    (merged,) = _mm_call("gated_merge", m, d, bm, bn, [h, o_attn, y_ssm], rhs, extras, [BF16], body,
                         vmem_mib=56)
    return merged


def _pad_time(x, lp):
    return jnp.pad(x, [(0, 0)] * (x.ndim - 1) + [(0, lp - x.shape[-1])])


def _layer(x, mod_rows, shared, dims, cache=None, ssm_h0=None):
    nb, t_len, d = x.shape
    nh, head_dim, n_groups, n_state = dims
    d_attn = nh * head_dim
    sample = cache is not None
    m = nb * t_len
    wts = shared
    raw = shared.pop("raw", None)
    ada = shared.pop("ada", None)
    if sample:
        xt = x.transpose(1, 0, 2).reshape(m, d)
        per, rpg = nb, m
        modg = lambda j: shared["mod"][mod_rows, j][None]
    else:
        xt = x.reshape(m, d)
        per, rpg = SUBLANES, t_len
        modg = lambda j: jnp.broadcast_to(shared["mod"][mod_rows, j][:, None, :], (nb, per, d))
    side = (lambda *names: [raw[k] for k in names]) if raw is not None else (lambda *names: [])

    h = _norm_mod_call(xt, wts["norm_g"][0], modg(1), modg(0), per, rpg)
    outs = _ffn_in_call(h, wts["ffn1_in"], side_casts=side("ffn1_out", "w_gate", "w_glu"), ada=ada)
    act = outs[0]
    if raw is not None:
        wts["ffn1_out"], wts["w_gate"], wts["w_glu"] = outs[1], outs[2], outs[3]
        w_in_bf = raw["w_in"].astype(BF16)
        wts["w_in"] = w_in_bf
        wts["w_f"] = jnp.pad(w_in_bf[:, 3 * d_attn:3 * d_attn + nh], ((0, 0), (0, LANES - nh)))
        wts["w_u"] = w_in_bf[:, 3 * d_attn + nh:]
    if ada is not None:
        n_all = shared["mod"].shape[0]
        shared["mod"] = jnp.concatenate([shared["mod"].reshape(n_all, -1), outs[-1]], axis=1).reshape(n_all, N_MOD, d)
    xt = _mm_res_call("ffn1_out", act, wts["ffn1_out"], xt, modg(2), HALF_STEP, per, rpg)

    h = _norm_mod_call(xt, wts["norm_g"][1], modg(4), modg(3), per, rpg)
    w_in_bf = wts["w_in"]
    outs = _proj_call("proj_q", h, w_in_bf, 0, d_attn, "q", gain=wts["q_norm_g"], head_dim=head_dim,
                      q_scale=head_dim ** -0.5 * LOG2E)
    q = outs[0]
    outs = _proj_call("proj_k", h, w_in_bf, d_attn, d_attn, "k", gain=wts["k_norm_g"], head_dim=head_dim)
    k32, kbf = outs[0], outs[1]
    outs = _proj_call("proj_v", h, w_in_bf, 2 * d_attn, d_attn, "v", w_f=wts["w_f"], b_f=wts["b_f"],
                      side_casts=side("w_attn_out"))
    v32, vbf, lf = outs[0], outs[1], outs[2]
    if raw is not None:
        wts["w_attn_out"] = outs[3]
    outs = _proj_call("proj_u", h, wts["w_u"], 0, wts["w_u"].shape[1], "u", side_casts=side("w_out"))
    u = outs[0]
    if raw is not None:
        wts["w_out"] = outs[1]
    logf = lf[:, :nh]

    ck = CUMSUM_CHUNK
    if sample:
        cache_k, cache_v, cache_logf = cache
        past = cache_k.shape[1]
        logf_bt = logf.reshape(t_len, nb, nh).transpose(1, 0, 2)
        lt = jnp.concatenate([cache_logf.astype(F32), logf_bt], axis=1).transpose(0, 2, 1)
        lp = -(-(past + t_len) // ck) * ck
        f_row = _cumsum_call(_pad_time(lt, lp))
        f_cache_row = f_row[:, :, None, :past]
        f_new_row = f_row[:, :, None, past:past + t_len]
        fq_col = f_row[:, :, past:past + t_len, None]
        to_bt = lambda a: a.reshape(t_len, nb, -1).transpose(1, 0, 2).reshape(m, -1)
        o_bt = _attn_sample_call(to_bt(q), to_bt(kbf), to_bt(vbf), cache_k, cache_v,
                                 fq_col, f_cache_row, f_new_row, nb, t_len, past, nh, head_dim)
        o_attn = o_bt.reshape(nb, t_len, d_attn).transpose(1, 0, 2).reshape(m, d_attn)
        k_out = to_bt(k32).reshape(nb, t_len, nh, head_dim)
        v_out = to_bt(v32).reshape(nb, t_len, nh, head_dim)
        logf_out = logf_bt
        u_tm = u
        h0_re = ssm_h0[0].astype(F32).reshape(nb, n_groups * n_state)
        h0_im = ssm_h0[1].astype(F32).reshape(nb, n_groups * n_state)
    else:
        lt = logf.reshape(nb, t_len, nh).transpose(0, 2, 1)
        lp = -(-t_len // ck) * ck
        f_row = _cumsum_call(_pad_time(lt, lp))[:, :, :t_len]
        f_col = f_row.transpose(0, 2, 1).reshape(m, nh)
        o_attn = _attn_prompt_call(q, kbf, vbf, f_col, f_row[:, :, None, :], nb, t_len, nh, head_dim)
        k_out = k32.reshape(nb, t_len, nh, head_dim)
        v_out = v32.reshape(nb, t_len, nh, head_dim)
        logf_out = logf.reshape(nb, t_len, nh)
        gather = nb == SUBLANES
        u_tm = u if gather else u.reshape(nb, t_len, -1).transpose(1, 0, 2).reshape(m, -1)
        h0_re = jnp.zeros((nb, n_groups * n_state), F32)
        h0_im = h0_re

    y_tm, h_re, h_im = _s5_call("s5_sample" if sample else "s5_prompt", u_tm, wts["s5_wb"], wts["s5_wc"],
                                wts["s5_a"], wts["s5_d"], h0_re, h0_im, nb, t_len,
                                batch_major=not sample and gather)
    if sample:
        y_ssm = y_tm
    else:
        y_ssm = y_tm.reshape(t_len, nb, -1).transpose(1, 0, 2).reshape(m, -1)

    merged = _merge_call(h, o_attn, y_ssm, wts["w_gate"], wts["b_gate"], wts["w_attn_out"], wts["w_glu"])
    xt = _mm_res_call("mix_out", merged, wts["w_out"], xt, modg(5), 1.0, per, rpg)

    h = _norm_mod_call(xt, wts["norm_g"][2], modg(7), modg(6), per, rpg)
    outs = _ffn_in_call(h, wts["ffn2_in"], side_casts=side("ffn2_out"))
    act = outs[0]
    if raw is not None:
        wts["ffn2_out"] = outs[1]
    xt = _mm_res_call("ffn2_out", act, wts["ffn2_out"], xt, modg(8), HALF_STEP, per, rpg)

    if sample:
        y = xt.reshape(t_len, nb, d).transpose(1, 0, 2)
    else:
        y = xt.reshape(nb, t_len, d)
    return (y, k_out, v_out, logf_out,
            h_re.reshape(nb, n_groups, n_state), h_im.reshape(nb, n_groups, n_state))


def kernel(x_prompt, x_sample, cache_k, cache_v, cache_logf, state_ssm_re, state_ssm_im, c_prompt, c_sample, w_ada, b_ada, norm_g, w_ffn1_in, w_ffn1_out, w_in, b_forget, q_norm_g, k_norm_g, w_attn_out, ssm_a_re, ssm_a_im, ssm_log_dt, ssm_b_re, ssm_b_im, ssm_c_re, ssm_c_im, ssm_d, w_glu, w_gate, b_gate, w_out, w_ffn2_in, w_ffn2_out):
    depth = w_ada.shape[0]
    nbp = x_prompt.shape[0]
    d = x_prompt.shape[2]
    nh, head_dim = cache_k.shape[3], cache_k.shape[4]
    n_groups, n_state, n_ch = ssm_b_re.shape[1:]
    d_attn = nh * head_dim
    d_ssm = n_groups * n_ch
    dims = (nh, head_dim, n_groups, n_state)
    assert head_dim == LANES and nh <= LANES
    assert S5_GROUPS_PER_TILE * n_ch == LANES and n_groups % S5_GROUPS_PER_TILE == 0

    xp, xs = x_prompt, x_sample
    outs_p, outs_s = [], []
    for l in range(depth):
        c_all = jnp.concatenate([c_prompt, c_sample], axis=0)
        b_ada_row = b_ada[l].reshape(1, -1)
        n_first = 2 * d
        mod_first = _ada_call(c_all, w_ada[l], b_ada_row, 0, n_first).reshape(c_all.shape[0], 2, d)
        wb, wc, a_pack = _s5_discretize(ssm_a_re[l].astype(F32), ssm_a_im[l].astype(F32), ssm_log_dt[l],
                                        ssm_b_re[l].astype(F32), ssm_b_im[l].astype(F32),
                                        ssm_c_re[l].astype(F32), ssm_c_im[l].astype(F32))
        shared = {
            "mod": mod_first,
            "ada": (c_all, w_ada[l], b_ada_row, n_first),
            "raw": {"ffn1_out": w_ffn1_out[l], "ffn2_out": w_ffn2_out[l], "w_in": w_in[l],
                    "w_gate": w_gate[l], "w_glu": w_glu[l], "w_attn_out": w_attn_out[l], "w_out": w_out[l]},
            "norm_g": norm_g[l].astype(F32),
            "ffn1_in": w_ffn1_in[l], "ffn2_in": w_ffn2_in[l],
            "b_f": jnp.pad(b_forget[l].astype(F32), (0, LANES - nh)).reshape(1, LANES),
            "q_norm_g": q_norm_g[l].astype(F32), "k_norm_g": k_norm_g[l].astype(F32),
            "b_gate": b_gate[l].astype(F32),
            "s5_wb": wb, "s5_wc": wc, "s5_a": a_pack, "s5_d": ssm_d[l].astype(F32).reshape(1, d_ssm),
        }
        xp, k1, v1, f1, r1, i1 = _layer(xp, slice(0, nbp), shared, dims)
        xs, k2, v2, f2, r2, i2 = _layer(xs, slice(nbp, None), shared, dims,
                                        cache=(cache_k[l], cache_v[l], cache_logf[l]),
                                        ssm_h0=(state_ssm_re[l], state_ssm_im[l]))
        outs_p.append((k1, v1, f1, r1, i1))
        outs_s.append((k2, v2, f2, r2, i2))

    stack = lambda outs, idx: jnp.stack([o[idx] for o in outs])
    return (xp, xs,
            stack(outs_p, 0), stack(outs_p, 1), stack(outs_p, 2), stack(outs_p, 3), stack(outs_p, 4),
            stack(outs_s, 0), stack(outs_s, 1), stack(outs_s, 2), stack(outs_s, 3), stack(outs_s, 4))
```

```python
import functools
import math

import jax
import jax.numpy as jnp
from jax import lax
from jax.experimental import pallas as pl
from jax.experimental.pallas import tpu as pltpu

F32 = jnp.float32
BF16 = jnp.bfloat16

EPS = 1e-6
HALF_STEP = 0.5
N_MOD = 9
NEG_BIG = -1e30
LOG2E = math.log2(math.e)
LANES = 128
SUBLANES = 8
MIB = 1024 * 1024
CUMSUM_CHUNK = 256
S5_GROUPS_PER_TILE = 8


def _params(n_axes, vmem_mib):
    return pltpu.CompilerParams(
        dimension_semantics=("arbitrary",) * n_axes,
        vmem_limit_bytes=vmem_mib * MIB,
    )


def _dot(a, b):
    return jnp.dot(a, b, preferred_element_type=F32)


def _dot_nt(a, b):
    return lax.dot_general(a, b, (((1,), (1,)), ((), ())), preferred_element_type=F32)


def _row_periodic(acc, per, fn):
    m, n = acc.shape
    return fn(acc.reshape(m // per, per, n)).reshape(m, n)


def _cast_rows(r, steps):
    tile = 2 * SUBLANES
    for rows in range(tile, r + 1, tile):
        if r % rows == 0 and r // rows <= steps:
            return rows
    return None


def _mm_call(name, m, n, bm, bn, lhs, rhs, extras, out_dtypes, body, vmem_mib=48, lhs_buffers=2,
             extra_outs=(), side_casts=()):
    assert m % bm == 0 and n % bn == 0, (name, m, n, bm, bn)
    nj = n // bn
    steps = (m // bm) * nj
    in_specs = []
    for a in lhs:
        mode = {} if lhs_buffers == 2 else {"pipeline_mode": pl.Buffered(lhs_buffers)}
        in_specs.append(pl.BlockSpec((bm, a.shape[1]), lambda i, j: (i, 0), **mode))
    for w, off in rhs:
        in_specs.append(pl.BlockSpec((w.shape[0], bn), lambda i, j, off=off: (0, j + off)))
    for _, bs, im in extras:
        in_specs.append(pl.BlockSpec(bs, im))
    out_specs = [pl.BlockSpec((bm, bn), lambda i, j: (i, j)) for _ in out_dtypes]
    out_shape = [jax.ShapeDtypeStruct((m, n), dt) for dt in out_dtypes]
    for shape, dt, bs, im in extra_outs:
        out_specs.append(pl.BlockSpec(bs, im))
        out_shape.append(jax.ShapeDtypeStruct(shape, dt))
    hosted = [arr for arr in side_casts if _cast_rows(arr.shape[0], steps) is not None]
    for arr in hosted:
        r, c = arr.shape
        rows = _cast_rows(r, steps)
        im = lambda i, j, last=r // rows - 1: (jnp.minimum(i * nj + j, last), 0)
        in_specs.append(pl.BlockSpec((rows, c), im))
        out_specs.append(pl.BlockSpec((rows, c), im))
        out_shape.append(jax.ShapeDtypeStruct((r, c), BF16))
    nl, nr, ne, ns = len(lhs), len(rhs), len(extras), len(hosted)
    n_in = nl + nr + ne + ns
    n_out = len(out_dtypes) + len(extra_outs)

    def kern(*refs):
        body(refs[:nl], refs[nl:nl + nr], refs[nl + nr:nl + nr + ne], refs[n_in:n_in + n_out])
        for s in range(ns):
            refs[n_in + n_out + s][...] = refs[nl + nr + ne + s][...].astype(BF16)

    outs = pl.pallas_call(
        kern,
        out_shape=out_shape,
        grid=(m // bm, nj),
        in_specs=in_specs,
        out_specs=out_specs,
        compiler_params=_params(2, vmem_mib),
        name=name,
    )(*lhs, *[w for w, _ in rhs], *[e for e, _, _ in extras], *hosted)
    outs = list(outs)
    main, cast_done = outs[:n_out], iter(outs[n_out:])
    casts = [next(cast_done) if any(arr is h_ for h_ in hosted) else arr.astype(BF16) for arr in side_casts]
    return main + casts


def _pick(total, prefs):
    for p in prefs:
        if total % p == 0:
            return p
    return total


def _ada_call(c, w_ada, b_ada, col0, n_cols):
    mb, d = c.shape
    bn = math.gcd(_pick(n_cols, (512, 256, 128)), col0) if col0 else _pick(n_cols, (512, 256, 128))
    assert bn % LANES == 0 and n_cols % bn == 0
    off = col0 // bn

    def kern(c_ref, w_ref, b_ref, o_ref):
        o_ref[...] = _dot(c_ref[...].astype(BF16), w_ref[...].astype(BF16)) + b_ref[...]

    return pl.pallas_call(
        kern,
        out_shape=jax.ShapeDtypeStruct((mb, n_cols), F32),
        grid=(n_cols // bn,),
        in_specs=[pl.BlockSpec((mb, d), lambda j: (0, 0)),
                  pl.BlockSpec((d, bn), lambda j: (0, j + off)),
                  pl.BlockSpec((1, bn), lambda j: (0, j + off))],
        out_specs=pl.BlockSpec((mb, bn), lambda j: (0, j)),
        compiler_params=_params(1, 40),
        name="ada_mod",
    )(c, w_ada, b_ada)


def _norm_mod_call(x, g, scale, shift, per, rows_per_group):
    m, d = x.shape
    bm = _pick(m, (512, 256, 128, 64, 32, 8))
    bm = min(bm, rows_per_group)

    rows = SUBLANES
    assert per % rows == 0 and bm % per == 0

    def kern(x_ref, g_ref, sc_ref, sh_ref, o_ref):
        for c in range(bm // rows):
            rs = slice(c * rows, (c + 1) * rows)
            ps = slice((c * rows) % per, (c * rows) % per + rows)
            xv = x_ref[rs, :]
            ms = jnp.mean(xv * xv, axis=-1, keepdims=True)
            y = xv * lax.rsqrt(ms + EPS) * g_ref[...]
            y = y * (1.0 + sc_ref[0, ps, :]) + sh_ref[0, ps, :]
            o_ref[rs, :] = y.astype(BF16)

    grp = lambda i: (i * bm // rows_per_group, 0, 0)
    return pl.pallas_call(
        kern,
        out_shape=jax.ShapeDtypeStruct((m, d), BF16),
        grid=(m // bm,),
        in_specs=[pl.BlockSpec((bm, d), lambda i: (i, 0)),
                  pl.BlockSpec((1, d), lambda i: (0, 0)),
                  pl.BlockSpec((1, per, d), grp),
                  pl.BlockSpec((1, per, d), grp)],
        out_specs=pl.BlockSpec((bm, d), lambda i: (i, 0)),
        compiler_params=_params(1, 40),
        name="norm_mod",
    )(x, g.reshape(1, d), scale, shift)


def _ffn_in_call(h, w_in, side_casts=(), ada=None):
    m, d = h.shape
    f = w_in.shape[1] // 2
    bm = _pick(m, (2048, 1024, 512, 256, 128, 64, 32, 8))
    bn = _pick(f, (256, 128))
    nj = f // bn
    steps = (m // bm) * nj
    extras, extra_outs = [], []
    ada_separate = None
    if ada is not None:
        c, w_ada, b_ada, col0 = ada
        n_rest = w_ada.shape[1] - col0
        assert col0 % LANES == 0 and n_rest % LANES == 0
        if n_rest // LANES > steps:
            ada_separate, ada = _ada_call(c, w_ada, b_ada, col0, n_rest), None
    if ada is not None:
        blk = lambda i, j: jnp.minimum(i * nj + j, n_rest // LANES - 1)
        extras = [(c, c.shape, lambda i, j: (0, 0)),
                  (w_ada, (w_ada.shape[0], LANES), lambda i, j: (0, col0 // LANES + blk(i, j))),
                  (b_ada, (1, LANES), lambda i, j: (0, col0 // LANES + blk(i, j)))]
        extra_outs = [((c.shape[0], n_rest), F32, (c.shape[0], LANES), lambda i, j: (0, blk(i, j)))]

    def body(a, w, e, o):
        hv = a[0][...]
        gate = _dot(hv, w[0][...].astype(BF16))
        up = _dot(hv, w[1][...].astype(BF16))
        o[0][...] = (gate * jax.nn.sigmoid(gate) * up).astype(BF16)
        if ada is not None:
            o[1][...] = _dot(e[0][...].astype(BF16), e[1][...].astype(BF16)) + e[2][...]

    outs = _mm_call("ffn_in", m, f, bm, bn, [h], [(w_in, 0), (w_in, nj)], extras, [BF16], body, vmem_mib=56,
                    lhs_buffers=1, extra_outs=extra_outs, side_casts=side_casts)
    if ada is not None:
        outs = [outs[0]] + outs[2:] + [outs[1]]
    elif ada_separate is not None:
        outs = outs + [ada_separate]
    return outs


def _mm_res_call(name, a, w_bf, x, gate, coef, per, rows_per_group):
    m, k = a.shape
    n = w_bf.shape[1]
    bm = _pick(m, (1024, 512, 256, 128, 64, 32, 8) if k <= 4096 else (512, 256, 128, 64, 32, 8))
    bm = min(bm, rows_per_group)
    bn = _pick(n, (1024, 512, 256, 128) if k <= 4096 else (512, 256, 128))

    def body(lhs, rhs, e, o):
        acc = _dot(lhs[0][...], rhs[0][...])
        r = _row_periodic(acc, per, lambda a3: a3 * (coef * (1.0 + e[1][...])))
        o[0][...] = e[0][...] + r

    extras = [(x, (bm, bn), lambda i, j: (i, j)),
              (gate, (1, per, bn), lambda i, j: (i * bm // rows_per_group, 0, j))]
    (out,) = _mm_call(name, m, n, bm, bn, [a], [(w_bf, 0)], extras, [F32], body, vmem_mib=56)
    return out


def _head_rms(acc, gain, head_dim):
    outs = []
    for hh in range(acc.shape[1] // head_dim):
        y = acc[:, hh * head_dim:(hh + 1) * head_dim]
        ms = jnp.mean(y * y, axis=-1, keepdims=True)
        outs.append(y * lax.rsqrt(ms + EPS) * gain)
    return outs


def _log_sigmoid(z):
    return jnp.minimum(z, 0.0) - jnp.log1p(jnp.exp(-jnp.abs(z)))


def _proj_call(name, h, w_bf, col_off, n, mode, gain=None, head_dim=LANES, q_scale=1.0, w_f=None, b_f=None,
               side_casts=()):
    m, k = h.shape
    bm = _pick(m, (1024, 512, 256, 128, 64, 32, 8))
    bn = _pick(n, (512, 256, 128) if mode == "v" else (1024, 512, 256, 128))
    extras, extra_outs = [], []
    if mode in ("q", "k"):
        extras.append((gain.reshape(1, head_dim), (1, head_dim), lambda i, j: (0, 0)))
    if mode == "v":
        extras.append((w_f, w_f.shape, lambda i, j: (0, 0)))
        extras.append((b_f, b_f.shape, lambda i, j: (0, 0)))
        extra_outs.append(((m, w_f.shape[1]), F32, (bm, w_f.shape[1]), lambda i, j: (i, 0)))
    out_dtypes = {"q": [BF16], "k": [F32, BF16], "v": [F32, BF16], "u": [F32]}[mode]

    def body(a, w, e, o):
        acc = _dot(a[0][...], w[0][...])
        if mode in ("q", "k"):
            parts = _head_rms(acc, e[0][...], head_dim)
            for hh, y in enumerate(parts):
                sl = slice(hh * head_dim, (hh + 1) * head_dim)
                if mode == "q":
                    o[0][:, sl] = (y * q_scale).astype(BF16)
                else:
                    o[0][:, sl] = y
                    o[1][:, sl] = y.astype(BF16)
        elif mode == "v":
            o[0][...] = acc
            o[1][...] = acc.astype(BF16)

            @pl.when(pl.program_id(1) == 0)
            def _():
                o[2][...] = _log_sigmoid(_dot(a[0][...], e[0][...]) + e[1][...])
        else:
            o[0][...] = acc

    assert col_off % bn == 0
    return _mm_call(name, m, n, bm, bn, [h], [(w_bf, col_off // bn)], extras, out_dtypes, body, vmem_mib=56,
                    extra_outs=extra_outs, side_casts=side_casts)


def _cumsum_call(lt):
    nb, nh, lp = lt.shape
    ck = CUMSUM_CHUNK
    assert lp % ck == 0
    rows = nb * nh
    rb = _pick(rows, (256, 128, 64, 32, 16, 8))

    def kern(x_ref, o_ref):
        row = lax.broadcasted_iota(jnp.int32, (ck, ck), 0)
        col = lax.broadcasted_iota(jnp.int32, (ck, ck), 1)
        upper = (row <= col).astype(F32)
        carry = jnp.zeros((rb, 1), F32)
        for c in range(lp // ck):
            xs = x_ref[:, c * ck:(c + 1) * ck]
            ys = jnp.dot(xs, upper, preferred_element_type=F32, precision=lax.Precision.HIGHEST) + carry
            o_ref[:, c * ck:(c + 1) * ck] = ys
            carry = ys[:, ck - 1:ck]

    out = pl.pallas_call(
        kern,
        out_shape=jax.ShapeDtypeStruct((rows, lp), F32),
        grid=(rows // rb,),
        in_specs=[pl.BlockSpec((rb, lp), lambda r: (r, 0))],
        out_specs=pl.BlockSpec((rb, lp), lambda r: (r, 0)),
        compiler_params=_params(1, 40),
        name="logf_cumsum",
    )(lt.reshape(rows, lp))
    return out.reshape(nb, nh, lp)


def _softmax_pv(parts, fq):
    m_keys = None
    for s, _ in parts:
        mx = jnp.max(s, axis=1, keepdims=True)
        m_keys = mx if m_keys is None else jnp.maximum(m_keys, mx)
    shift = fq - (m_keys + fq)
    l_sum, acc = None, None
    for s, v in parts:
        p = jnp.exp2(s + shift)
        ls = jnp.sum(p, axis=1, keepdims=True)
        pv = _dot(p.astype(BF16), v)
        l_sum = ls if l_sum is None else l_sum + ls
        acc = pv if acc is None else acc + pv
    return acc / l_sum


def _causal_mask(s):
    r = lax.broadcasted_iota(jnp.int32, s.shape, 0)
    c = lax.broadcasted_iota(jnp.int32, s.shape, 1)
    return jnp.where(c <= r, s, NEG_BIG)


def _attn_prompt_call(q, k, v, f_col, f_row, nb, seq, nh, head_dim):
    tq = _pick(seq, (512, 256, 128, 64, 32, 8))
    nq = seq // tq
    hp = 2 if nh % 2 == 0 else 1
    hw = hp * head_dim

    def kern(q_ref, k_ref, v_ref, fc_ref, fr_ref, o_ref, fq_ref):
        fall = fc_ref[...]
        lane = lax.broadcasted_iota(jnp.int32, fall.shape, 1)
        for j in range(hp):
            hh = pl.program_id(1) * hp + j
            fq_ref[j] = jnp.sum(jnp.where(lane == hh, fall, 0.0), axis=1, keepdims=True) * LOG2E
        for qi in range(nq):
            lo, hi = qi * tq, (qi + 1) * tq
            for j in range(hp):
                cs = slice(j * head_dim, (j + 1) * head_dim)
                qv = q_ref[lo:hi, cs]
                s_d = _causal_mask(_dot_nt(qv, k_ref[lo:hi, cs]) - fr_ref[0, j, :, lo:hi] * LOG2E)
                parts = [(s_d, v_ref[lo:hi, cs])]
                if qi > 0:
                    s_o = _dot_nt(qv, k_ref[0:lo, cs]) - fr_ref[0, j, :, 0:lo] * LOG2E
                    parts.append((s_o, v_ref[0:lo, cs]))
                o_ref[lo:hi, cs] = _softmax_pv(parts, fq_ref[j, lo:hi, :]).astype(BF16)

    return pl.pallas_call(
        kern,
        out_shape=jax.ShapeDtypeStruct((nb * seq, nh * head_dim), BF16),
        grid=(nb, nh // hp),
        in_specs=[pl.BlockSpec((seq, hw), lambda b, h: (b, h)),
                  pl.BlockSpec((seq, hw), lambda b, h: (b, h)),
                  pl.BlockSpec((seq, hw), lambda b, h: (b, h)),
                  pl.BlockSpec((seq, nh), lambda b, h: (b, 0)),
                  pl.BlockSpec((1, hp, 1, seq), lambda b, h: (b, h, 0, 0))],
        out_specs=pl.BlockSpec((seq, hw), lambda b, h: (b, h)),
        scratch_shapes=[pltpu.VMEM((hp, seq, 1), F32)],
        compiler_params=_params(2, 40),
        name="attn_prompt",
    )(q, k, v, f_col, f_row)


def _attn_sample_call(q, k_new, v_new, cache_k, cache_v, fq_col, f_cache_row, f_new_row,
                      nb, s_len, past, nh, head_dim):
    n_steps = nb * nh
    n_slots = 3
    n_parts = 2 if past % (2 * SUBLANES) == 0 else 1
    part = past // n_parts

    def kern(q_ref, kn_ref, vn_ref, fq_ref, fc_ref, fn_ref, ck_hbm, cv_hbm, o_ref, kbuf, vbuf, sem):
        step = pl.program_id(0) * nh + pl.program_id(1)
        slot = lax.rem(step, n_slots)

        def copies(s_idx, slot_idx):
            bb = s_idx // nh
            hh = lax.rem(s_idx, nh)
            out = []
            for j in range(n_parts):
                rs = pl.ds(j * part, part)
                out.append(pltpu.make_async_copy(ck_hbm.at[bb, rs, hh, :], kbuf.at[slot_idx, rs],
                                                 sem.at[0, slot_idx, j]))
                out.append(pltpu.make_async_copy(cv_hbm.at[bb, rs, hh, :], vbuf.at[slot_idx, rs],
                                                 sem.at[1, slot_idx, j]))
            return out

        @pl.when(step == 0)
        def _():
            for ahead in range(min(n_slots - 1, n_steps)):
                for cp in copies(step + ahead, ahead):
                    cp.start()

        @pl.when(step + (n_slots - 1) < n_steps)
        def _():
            for cp in copies(step + (n_slots - 1), lax.rem(step + (n_slots - 1), n_slots)):
                cp.start()

        for cp in copies(step, slot):
            cp.wait()

        qv = q_ref[...]
        s_c = _dot_nt(qv, kbuf[slot].astype(BF16)) - fc_ref[0, 0] * LOG2E
        s_n = _causal_mask(_dot_nt(qv, kn_ref[...]) - fn_ref[0, 0] * LOG2E)
        parts = [(s_c, vbuf[slot].astype(BF16)), (s_n, vn_ref[...])]
        o_ref[...] = _softmax_pv(parts, fq_ref[0, 0] * LOG2E).astype(BF16)

    return pl.pallas_call(
        kern,
        out_shape=jax.ShapeDtypeStruct((nb * s_len, nh * head_dim), BF16),
        grid=(nb, nh),
        in_specs=[pl.BlockSpec((s_len, head_dim), lambda b, h: (b, h)),
                  pl.BlockSpec((s_len, head_dim), lambda b, h: (b, h)),
                  pl.BlockSpec((s_len, head_dim), lambda b, h: (b, h)),
                  pl.BlockSpec((1, 1, s_len, 1), lambda b, h: (b, h, 0, 0)),
                  pl.BlockSpec((1, 1, 1, past), lambda b, h: (b, h, 0, 0)),
                  pl.BlockSpec((1, 1, 1, s_len), lambda b, h: (b, h, 0, 0)),
                  pl.BlockSpec(memory_space=pl.ANY),
                  pl.BlockSpec(memory_space=pl.ANY)],
        out_specs=pl.BlockSpec((s_len, head_dim), lambda b, h: (b, h)),
        scratch_shapes=[pltpu.VMEM((n_slots, past, head_dim), F32),
                        pltpu.VMEM((n_slots, past, head_dim), F32),
                        pltpu.SemaphoreType.DMA((2, n_slots, n_parts))],
        compiler_params=_params(2, 40),
        name="attn_sample",
    )(q, k_new, v_new, fq_col, f_cache_row, f_new_row, cache_k, cache_v)


def _s5_call(name, u, wb, wc, a_pack, d_row, h0_re, h0_im, nb, t_len, batch_major=False):
    rows, d_ssm = u.shape
    n_gt = wb.shape[0]
    cw = wb.shape[1]
    sw = wb.shape[2] // 2
    nt = 2 if n_gt % 2 == 0 else 1
    tc = _pick(t_len, tuple(c for c in (128, 64, 32, 16, 8, 4, 2, 1) if c * nb <= 1024))
    n_chunks = t_len // tc
    rc = tc * nb
    assert nb % SUBLANES == 0
    assert not batch_major or nb == SUBLANES
    n_steps = (n_gt // nt) * n_chunks
    uw = nt * cw

    def kern(u_ref, wb_ref, wc_ref, a_ref, d_ref, h0r_ref, h0i_ref, y_ref, hr_out, hi_out,
             hr_ref, hi_ref, bu_ref, *gather_scratch):
        c = pl.program_id(1)
        if batch_major:
            ubuf, sem = gather_scratch
            step = pl.program_id(0) * n_chunks + c
            slot = lax.rem(step, 2)

            def copies(s_idx, slot_idx):
                t0 = pl.multiple_of(lax.rem(s_idx, n_chunks) * tc, tc)
                c0 = pl.multiple_of((s_idx // n_chunks) * uw, uw)
                return [pltpu.make_async_copy(u_ref.at[b, pl.ds(t0, tc), pl.ds(c0, uw)],
                                              ubuf.at[slot_idx, :, b, :], sem.at[slot_idx, b])
                        for b in range(nb)]

            @pl.when(step == 0)
            def _():
                for cp in copies(step, slot):
                    cp.start()

            @pl.when(step + 1 < n_steps)
            def _():
                for cp in copies(step + 1, 1 - slot):
                    cp.start()

            for cp in copies(step, slot):
                cp.wait()
            u_all = ubuf[slot].reshape(rc, uw)
            u_tile = lambda g: u_all[:, g * cw:(g + 1) * cw]
        else:
            u_tile = lambda g: u_ref[:, g * cw:(g + 1) * cw]

        @pl.when(c == 0)
        def _():
            hr_ref[...] = h0r_ref[...]
            hi_ref[...] = h0i_ref[...]

        for g in range(nt):
            bu_ref[g] = _dot(u_tile(g).astype(BF16), wb_ref[g])
        for g in range(nt):
            ar = jnp.broadcast_to(a_ref[g, 0:1, :], (SUBLANES, sw))
            ai = jnp.broadcast_to(a_ref[g, 1:2, :], (SUBLANES, sw))
            for r in range(nb // SUBLANES):
                rs = slice(r * SUBLANES, (r + 1) * SUBLANES)
                hr = hr_ref[rs, g * sw:(g + 1) * sw]
                hi = hi_ref[rs, g * sw:(g + 1) * sw]
                for t in range(tc):
                    ts = slice(t * nb + r * SUBLANES, t * nb + (r + 1) * SUBLANES)
                    nhr = ar * hr - ai * hi + bu_ref[g, ts, 0:sw]
                    nhi = ar * hi + ai * hr + bu_ref[g, ts, sw:2 * sw]
                    bu_ref[g, ts, 0:sw] = nhr
                    bu_ref[g, ts, sw:2 * sw] = nhi
                    hr, hi = nhr, nhi
                hr_ref[rs, g * sw:(g + 1) * sw] = hr
                hi_ref[rs, g * sw:(g + 1) * sw] = hi
            cs = slice(g * cw, (g + 1) * cw)
            y = _dot(bu_ref[g].astype(BF16), wc_ref[g]) + d_ref[:, cs] * u_tile(g)
            y_ref[:, cs] = jax.nn.gelu(y, approximate=True).astype(BF16)

        @pl.when(c == n_chunks - 1)
        def _():
            hr_out[...] = hr_ref[...]
            hi_out[...] = hi_ref[...]

    return pl.pallas_call(
        kern,
        out_shape=[jax.ShapeDtypeStruct((rows, d_ssm), BF16),
                   jax.ShapeDtypeStruct((nb, n_gt * sw), F32),
                   jax.ShapeDtypeStruct((nb, n_gt * sw), F32)],
        grid=(n_gt // nt, n_chunks),
        in_specs=[pl.BlockSpec(memory_space=pl.ANY) if batch_major else
                  pl.BlockSpec((rc, nt * cw), lambda g, c: (c, g)),
                  pl.BlockSpec((nt, cw, 2 * sw), lambda g, c: (g, 0, 0)),
                  pl.BlockSpec((nt, 2 * sw, cw), lambda g, c: (g, 0, 0)),
                  pl.BlockSpec((nt, 2, sw), lambda g, c: (g, 0, 0)),
                  pl.BlockSpec((1, nt * cw), lambda g, c: (0, g)),
                  pl.BlockSpec((nb, nt * sw), lambda g, c: (0, g)),
                  pl.BlockSpec((nb, nt * sw), lambda g, c: (0, g))],
        out_specs=[pl.BlockSpec((rc, nt * cw), lambda g, c: (c, g)),
                   pl.BlockSpec((nb, nt * sw), lambda g, c: (0, g)),
                   pl.BlockSpec((nb, nt * sw), lambda g, c: (0, g))],
        scratch_shapes=[pltpu.VMEM((nb, nt * sw), F32),
                        pltpu.VMEM((nb, nt * sw), F32),
                        pltpu.VMEM((nt, rc, 2 * sw), F32)] + (
            [pltpu.VMEM((2, tc, nb, uw), F32), pltpu.SemaphoreType.DMA((2, nb))] if batch_major else []),
        compiler_params=_params(2, 40),
        name=name,
    )(u.reshape(nb, t_len, d_ssm) if batch_major else u, wb, wc, a_pack, d_row, h0_re, h0_im)


def _s5_discretize(a_re, a_im, log_dt, b_re, b_im, c_re, c_im):
    g, p = a_re.shape
    ch = b_re.shape[2]
    gpt = S5_GROUPS_PER_TILE
    n_gt = g // gpt
    dt = jnp.exp(log_dt.astype(F32))[:, None]
    mag = jnp.exp(a_re * dt)
    abar_re = mag * jnp.cos(a_im * dt)
    abar_im = mag * jnp.sin(a_im * dt)
    xr = abar_re - 1
    den = a_re * a_re + a_im * a_im
    coef_re = ((xr * a_re + abar_im * a_im) / den)[..., None]
    coef_im = ((abar_im * a_re - xr * a_im) / den)[..., None]
    bb_re = coef_re * b_re - coef_im * b_im
    bb_im = coef_re * b_im + coef_im * b_re
    eye = jnp.eye(gpt, dtype=F32)

    def blockdiag_in(bb):
        t = bb.reshape(n_gt, gpt, p, ch)
        return jnp.einsum("tgpc,gh->tgchp", t, eye).reshape(n_gt, gpt * ch, gpt * p)

    def blockdiag_out(cc):
        t = cc.reshape(n_gt, gpt, ch, p)
        return jnp.einsum("tgcp,gh->tgphc", t, eye).reshape(n_gt, gpt * p, gpt * ch)

    wb = jnp.concatenate([blockdiag_in(bb_re), blockdiag_in(bb_im)], axis=2).astype(BF16)
    wc = jnp.concatenate([blockdiag_out(c_re), -blockdiag_out(c_im)], axis=1).astype(BF16)
    a_pack = jnp.stack([abar_re.reshape(n_gt, gpt * p), abar_im.reshape(n_gt, gpt * p)], axis=1)
    return wb, wc, a_pack


def _merge_call(h, o_attn, y_ssm, w_gate_bf, b_gate, w_attn_out_bf, w_glu_bf):
    m, d = h.shape
    bm = _pick(m, (512, 256, 128, 64, 32, 8))
    bn = _pick(d, (512, 256, 128))
    nblk = d // bn

    def body(a, w, e, o):
        hv = a[0][...]
        ga = jax.nn.sigmoid(_dot(hv, w[0][...]) + e[0][...])
        gs = jax.nn.sigmoid(_dot(hv, w[1][...]) + e[1][...])
        ab = _dot(a[1][...], w[2][...])
        yv = a[2][...]
        sb = _dot(yv, w[3][...]) * jax.nn.sigmoid(_dot(yv, w[4][...]))
        o[0][...] = (ga * ab + gs * sb).astype(BF16)

    bg = b_gate.reshape(1, 2 * d)
    extras = [(bg, (1, bn), lambda i, j: (0, j)),
              (bg, (1, bn), lambda i, j: (0, j + nblk))]
    rhs = [(w_gate_bf, 0), (w_gate_bf, nblk), (w_attn_out_bf, 0), (w_glu_bf, 0), (w_glu_bf, nblk)]
    (merged,) = _mm_call("gated_merge", m, d, bm, bn, [h, o_attn, y_ssm], rhs, extras, [BF16], body,
                         vmem_mib=56)
    return merged


def _pad_time(x, lp):
    return jnp.pad(x, [(0, 0)] * (x.ndim - 1) + [(0, lp - x.shape[-1])])


def _layer(x, mod_rows, shared, dims, cache=None, ssm_h0=None):
    nb, t_len, d = x.shape
    nh, head_dim, n_groups, n_state = dims
    d_attn = nh * head_dim
    sample = cache is not None
    m = nb * t_len
    wts = shared
    raw = shared.pop("raw", None)
    ada = shared.pop("ada", None)
    if sample:
        xt = x.transpose(1, 0, 2).reshape(m, d)
        per, rpg = nb, m
        modg = lambda j: shared["mod"][mod_rows, j][None]
    else:
        xt = x.reshape(m, d)
        per, rpg = SUBLANES, t_len
        modg = lambda j: jnp.broadcast_to(shared["mod"][mod_rows, j][:, None, :], (nb, per, d))
    side = (lambda *names: [raw[k] for k in names]) if raw is not None else (lambda *names: [])

    h = _norm_mod_call(xt, wts["norm_g"][0], modg(1), modg(0), per, rpg)
    outs = _ffn_in_call(h, wts["ffn1_in"], side_casts=side("ffn1_out", "w_gate", "w_glu"), ada=ada)
    act = outs[0]
    if raw is not None:
        wts["ffn1_out"], wts["w_gate"], wts["w_glu"] = outs[1], outs[2], outs[3]
        w_in_bf = raw["w_in"].astype(BF16)
        wts["w_in"] = w_in_bf
        wts["w_f"] = jnp.pad(w_in_bf[:, 3 * d_attn:3 * d_attn + nh], ((0, 0), (0, LANES - nh)))
        wts["w_u"] = w_in_bf[:, 3 * d_attn + nh:]
    if ada is not None:
        n_all = shared["mod"].shape[0]
        shared["mod"] = jnp.concatenate([shared["mod"].reshape(n_all, -1), outs[-1]], axis=1).reshape(n_all, N_MOD, d)
    xt = _mm_res_call("ffn1_out", act, wts["ffn1_out"], xt, modg(2), HALF_STEP, per, rpg)

    h = _norm_mod_call(xt, wts["norm_g"][1], modg(4), modg(3), per, rpg)
    w_in_bf = wts["w_in"]
    outs = _proj_call("proj_q", h, w_in_bf, 0, d_attn, "q", gain=wts["q_norm_g"], head_dim=head_dim,
                      q_scale=head_dim ** -0.5 * LOG2E)
    q = outs[0]
    outs = _proj_call("proj_k", h, w_in_bf, d_attn, d_attn, "k", gain=wts["k_norm_g"], head_dim=head_dim)
    k32, kbf = outs[0], outs[1]
    outs = _proj_call("proj_v", h, w_in_bf, 2 * d_attn, d_attn, "v", w_f=wts["w_f"], b_f=wts["b_f"],
                      side_casts=side("w_attn_out"))
    v32, vbf, lf = outs[0], outs[1], outs[2]
    if raw is not None:
        wts["w_attn_out"] = outs[3]
    outs = _proj_call("proj_u", h, wts["w_u"], 0, wts["w_u"].shape[1], "u", side_casts=side("w_out"))
    u = outs[0]
    if raw is not None:
        wts["w_out"] = outs[1]
    logf = lf[:, :nh]

    ck = CUMSUM_CHUNK
    if sample:
        cache_k, cache_v, cache_logf = cache
        past = cache_k.shape[1]
        logf_bt = logf.reshape(t_len, nb, nh).transpose(1, 0, 2)
        lt = jnp.concatenate([cache_logf.astype(F32), logf_bt], axis=1).transpose(0, 2, 1)
        lp = -(-(past + t_len) // ck) * ck
        f_row = _cumsum_call(_pad_time(lt, lp))
        f_cache_row = f_row[:, :, None, :past]
        f_new_row = f_row[:, :, None, past:past + t_len]
        fq_col = f_row[:, :, past:past + t_len, None]
        to_bt = lambda a: a.reshape(t_len, nb, -1).transpose(1, 0, 2).reshape(m, -1)
        o_bt = _attn_sample_call(to_bt(q), to_bt(kbf), to_bt(vbf), cache_k, cache_v,
                                 fq_col, f_cache_row, f_new_row, nb, t_len, past, nh, head_dim)
        o_attn = o_bt.reshape(nb, t_len, d_attn).transpose(1, 0, 2).reshape(m, d_attn)
        k_out = to_bt(k32).reshape(nb, t_len, nh, head_dim)
        v_out = to_bt(v32).reshape(nb, t_len, nh, head_dim)
        logf_out = logf_bt
        u_tm = u
        h0_re = ssm_h0[0].astype(F32).reshape(nb, n_groups * n_state)
        h0_im = ssm_h0[1].astype(F32).reshape(nb, n_groups * n_state)
    else:
        lt = logf.reshape(nb, t_len, nh).transpose(0, 2, 1)
        lp = -(-t_len // ck) * ck
        f_row = _cumsum_call(_pad_time(lt, lp))[:, :, :t_len]
        f_col = f_row.transpose(0, 2, 1).reshape(m, nh)
        o_attn = _attn_prompt_call(q, kbf, vbf, f_col, f_row[:, :, None, :], nb, t_len, nh, head_dim)
        k_out = k32.reshape(nb, t_len, nh, head_dim)
        v_out = v32.reshape(nb, t_len, nh, head_dim)
        logf_out = logf.reshape(nb, t_len, nh)
        gather = nb == SUBLANES
        u_tm = u if gather else u.reshape(nb, t_len, -1).transpose(1, 0, 2).reshape(m, -1)
        h0_re = jnp.zeros((nb, n_groups * n_state), F32)
        h0_im = h0_re

    y_tm, h_re, h_im = _s5_call("s5_sample" if sample else "s5_prompt", u_tm, wts["s5_wb"], wts["s5_wc"],
                                wts["s5_a"], wts["s5_d"], h0_re, h0_im, nb, t_len,
                                batch_major=not sample and gather)
    if sample:
        y_ssm = y_tm
    else:
        y_ssm = y_tm.reshape(t_len, nb, -1).transpose(1, 0, 2).reshape(m, -1)

    merged = _merge_call(h, o_attn, y_ssm, wts["w_gate"], wts["b_gate"], wts["w_attn_out"], wts["w_glu"])
    xt = _mm_res_call("mix_out", merged, wts["w_out"], xt, modg(5), 1.0, per, rpg)

    h = _norm_mod_call(xt, wts["norm_g"][2], modg(7), modg(6), per, rpg)
    outs = _ffn_in_call(h, wts["ffn2_in"], side_casts=side("ffn2_out"))
    act = outs[0]
    if raw is not None:
        wts["ffn2_out"] = outs[1]
    xt = _mm_res_call("ffn2_out", act, wts["ffn2_out"], xt, modg(8), HALF_STEP, per, rpg)

    if sample:
        y = xt.reshape(t_len, nb, d).transpose(1, 0, 2)
    else:
        y = xt.reshape(nb, t_len, d)
    return (y, k_out, v_out, logf_out,
            h_re.reshape(nb, n_groups, n_state), h_im.reshape(nb, n_groups, n_state))


def kernel(x_prompt, x_sample, cache_k, cache_v, cache_logf, state_ssm_re, state_ssm_im, c_prompt, c_sample, w_ada, b_ada, norm_g, w_ffn1_in, w_ffn1_out, w_in, b_forget, q_norm_g, k_norm_g, w_attn_out, ssm_a_re, ssm_a_im, ssm_log_dt, ssm_b_re, ssm_b_im, ssm_c_re, ssm_c_im, ssm_d, w_glu, w_gate, b_gate, w_out, w_ffn2_in, w_ffn2_out):
    depth = w_ada.shape[0]
    nbp = x_prompt.shape[0]
    d = x_prompt.shape[2]
    nh, head_dim = cache_k.shape[3], cache_k.shape[4]
    n_groups, n_state, n_ch = ssm_b_re.shape[1:]
    d_attn = nh * head_dim
    d_ssm = n_groups * n_ch
    dims = (nh, head_dim, n_groups, n_state)
    assert head_dim == LANES and nh <= LANES
    assert S5_GROUPS_PER_TILE * n_ch == LANES and n_groups % S5_GROUPS_PER_TILE == 0

    xp, xs = x_prompt, x_sample
    outs_p, outs_s = [], []
    for l in range(depth):
        c_all = jnp.concatenate([c_prompt, c_sample], axis=0)
        b_ada_row = b_ada[l].reshape(1, -1)
        n_first = 2 * d
        mod_first = _ada_call(c_all, w_ada[l], b_ada_row, 0, n_first).reshape(c_all.shape[0], 2, d)
        wb, wc, a_pack = _s5_discretize(ssm_a_re[l].astype(F32), ssm_a_im[l].astype(F32), ssm_log_dt[l],
                                        ssm_b_re[l].astype(F32), ssm_b_im[l].astype(F32),
                                        ssm_c_re[l].astype(F32), ssm_c_im[l].astype(F32))
        shared = {
            "mod": mod_first,
            "ada": (c_all, w_ada[l], b_ada_row, n_first),
            "raw": {"ffn1_out": w_ffn1_out[l], "ffn2_out": w_ffn2_out[l], "w_in": w_in[l],
                    "w_gate": w_gate[l], "w_glu": w_glu[l], "w_attn_out": w_attn_out[l], "w_out": w_out[l]},
            "norm_g": norm_g[l].astype(F32),
            "ffn1_in": w_ffn1_in[l], "ffn2_in": w_ffn2_in[l],
            "b_f": jnp.pad(b_forget[l].astype(F32), (0, LANES - nh)).reshape(1, LANES),
            "q_norm_g": q_norm_g[l].astype(F32), "k_norm_g": k_norm_g[l].astype(F32),
            "b_gate": b_gate[l].astype(F32),
            "s5_wb": wb, "s5_wc": wc, "s5_a": a_pack, "s5_d": ssm_d[l].astype(F32).reshape(1, d_ssm),
        }
        xp, k1, v1, f1, r1, i1 = _layer(xp, slice(0, nbp), shared, dims)
        xs, k2, v2, f2, r2, i2 = _layer(xs, slice(nbp, None), shared, dims,
                                        cache=(cache_k[l], cache_v[l], cache_logf[l]),
                                        ssm_h0=(state_ssm_re[l], state_ssm_im[l]))
        outs_p.append((k1, v1, f1, r1, i1))
        outs_s.append((k2, v2, f2, r2, i2))

    stack = lambda outs, idx: jnp.stack([o[idx] for o in outs])
    return (xp, xs,
            stack(outs_p, 0), stack(outs_p, 1), stack(outs_p, 2), stack(outs_p, 3), stack(outs_p, 4),
            stack(outs_s, 0), stack(outs_s, 1), stack(outs_s, 2), stack(outs_s, 3), stack(outs_s, 4))
```

```python
import functools
import math

import jax
import jax.numpy as jnp
from jax import lax
from jax.experimental import pallas as pl
from jax.experimental.pallas import tpu as pltpu

F32 = jnp.float32
BF16 = jnp.bfloat16

EPS = 1e-6
HALF_STEP = 0.5
N_MOD = 9
NEG_BIG = -1e30
LOG2E = math.log2(math.e)
LANES = 128
SUBLANES = 8
MIB = 1024 * 1024
CUMSUM_CHUNK = 256
S5_GROUPS_PER_TILE = 8


def _params(n_axes, vmem_mib):
    return pltpu.CompilerParams(
        dimension_semantics=("arbitrary",) * n_axes,
        vmem_limit_bytes=vmem_mib * MIB,
    )


def _dot(a, b):
    return jnp.dot(a, b, preferred_element_type=F32)


def _dot_nt(a, b):
    return lax.dot_general(a, b, (((1,), (1,)), ((), ())), preferred_element_type=F32)


def _row_periodic(acc, per, fn):
    m, n = acc.shape
    return fn(acc.reshape(m // per, per, n)).reshape(m, n)


def _cast_rows(r, steps):
    tile = 2 * SUBLANES
    for rows in range(tile, r + 1, tile):
        if r % rows == 0 and r // rows <= steps:
            return rows
    return None


def _mm_call(name, m, n, bm, bn, lhs, rhs, extras, out_dtypes, body, vmem_mib=48, lhs_buffers=2,
             extra_outs=(), side_casts=()):
    assert m % bm == 0 and n % bn == 0, (name, m, n, bm, bn)
    nj = n // bn
    steps = (m // bm) * nj
    in_specs = []
    for a in lhs:
        mode = {} if lhs_buffers == 2 else {"pipeline_mode": pl.Buffered(lhs_buffers)}
        in_specs.append(pl.BlockSpec((bm, a.shape[1]), lambda i, j: (i, 0), **mode))
    for w, off in rhs:
        in_specs.append(pl.BlockSpec((w.shape[0], bn), lambda i, j, off=off: (0, j + off)))
    for _, bs, im in extras:
        in_specs.append(pl.BlockSpec(bs, im))
    out_specs = [pl.BlockSpec((bm, bn), lambda i, j: (i, j)) for _ in out_dtypes]
    out_shape = [jax.ShapeDtypeStruct((m, n), dt) for dt in out_dtypes]
    for shape, dt, bs, im in extra_outs:
        out_specs.append(pl.BlockSpec(bs, im))
        out_shape.append(jax.ShapeDtypeStruct(shape, dt))
    hosted = [arr for arr in side_casts if _cast_rows(arr.shape[0], steps) is not None]
    for arr in hosted:
        r, c = arr.shape
        rows = _cast_rows(r, steps)
        im = lambda i, j, last=r // rows - 1: (jnp.minimum(i * nj + j, last), 0)
        in_specs.append(pl.BlockSpec((rows, c), im))
        out_specs.append(pl.BlockSpec((rows, c), im))
        out_shape.append(jax.ShapeDtypeStruct((r, c), BF16))
    nl, nr, ne, ns = len(lhs), len(rhs), len(extras), len(hosted)
    n_in = nl + nr + ne + ns
    n_out = len(out_dtypes) + len(extra_outs)

    def kern(*refs):
        body(refs[:nl], refs[nl:nl + nr], refs[nl + nr:nl + nr + ne], refs[n_in:n_in + n_out])
        for s in range(ns):
            refs[n_in + n_out + s][...] = refs[nl + nr + ne + s][...].astype(BF16)

    outs = pl.pallas_call(
        kern,
        out_shape=out_shape,
        grid=(m // bm, nj),
        in_specs=in_specs,
        out_specs=out_specs,
        compiler_params=_params(2, vmem_mib),
        name=name,
    )(*lhs, *[w for w, _ in rhs], *[e for e, _, _ in extras], *hosted)
    outs = list(outs)
    main, cast_done = outs[:n_out], iter(outs[n_out:])
    casts = [next(cast_done) if any(arr is h_ for h_ in hosted) else arr.astype(BF16) for arr in side_casts]
    return main + casts


def _pick(total, prefs):
    for p in prefs:
        if total % p == 0:
            return p
    return total


def _ada_call(c, w_ada, b_ada, col0, n_cols):
    mb, d = c.shape
    bn = math.gcd(_pick(n_cols, (512, 256, 128)), col0) if col0 else _pick(n_cols, (512, 256, 128))
    assert bn % LANES == 0 and n_cols % bn == 0
    off = col0 // bn

    def kern(c_ref, w_ref, b_ref, o_ref):
        o_ref[...] = _dot(c_ref[...].astype(BF16), w_ref[...].astype(BF16)) + b_ref[...]

    return pl.pallas_call(
        kern,
        out_shape=jax.ShapeDtypeStruct((mb, n_cols), F32),
        grid=(n_cols // bn,),
        in_specs=[pl.BlockSpec((mb, d), lambda j: (0, 0)),
                  pl.BlockSpec((d, bn), lambda j: (0, j + off)),
                  pl.BlockSpec((1, bn), lambda j: (0, j + off))],
        out_specs=pl.BlockSpec((mb, bn), lambda j: (0, j)),
        compiler_params=_params(1, 40),
        name="ada_mod",
    )(c, w_ada, b_ada)


def _norm_mod_call(x, g, scale, shift, per, rows_per_group):
    m, d = x.shape
    bm = _pick(m, (512, 256, 128, 64, 32, 8))
    bm = min(bm, rows_per_group)

    rows = SUBLANES
    assert per % rows == 0 and bm % per == 0

    def kern(x_ref, g_ref, sc_ref, sh_ref, o_ref):
        for c in range(bm // rows):
            rs = slice(c * rows, (c + 1) * rows)
            ps = slice((c * rows) % per, (c * rows) % per + rows)
            xv = x_ref[rs, :]
            ms = jnp.mean(xv * xv, axis=-1, keepdims=True)
            y = xv * lax.rsqrt(ms + EPS) * g_ref[...]
            y = y * (1.0 + sc_ref[0, ps, :]) + sh_ref[0, ps, :]
            o_ref[rs, :] = y.astype(BF16)

    grp = lambda i: (i * bm // rows_per_group, 0, 0)
    return pl.pallas_call(
        kern,
        out_shape=jax.ShapeDtypeStruct((m, d), BF16),
        grid=(m // bm,),
        in_specs=[pl.BlockSpec((bm, d), lambda i: (i, 0)),
                  pl.BlockSpec((1, d), lambda i: (0, 0)),
                  pl.BlockSpec((1, per, d), grp),
                  pl.BlockSpec((1, per, d), grp)],
        out_specs=pl.BlockSpec((bm, d), lambda i: (i, 0)),
        compiler_params=_params(1, 40),
        name="norm_mod",
    )(x, g.reshape(1, d), scale, shift)


def _ffn_in_call(h, w_in, side_casts=(), ada=None):
    m, d = h.shape
    f = w_in.shape[1] // 2
    bm = _pick(m, (2048, 1024, 512, 256, 128, 64, 32, 8))
    bn = _pick(f, (256, 128))
    nj = f // bn
    steps = (m // bm) * nj
    extras, extra_outs = [], []
    ada_separate = None
    if ada is not None:
        c, w_ada, b_ada, col0 = ada
        n_rest = w_ada.shape[1] - col0
        assert col0 % LANES == 0 and n_rest % LANES == 0
        if n_rest // LANES > steps:
            ada_separate, ada = _ada_call(c, w_ada, b_ada, col0, n_rest), None
    if ada is not None:
        blk = lambda i, j: jnp.minimum(i * nj + j, n_rest // LANES - 1)
        extras = [(c, c.shape, lambda i, j: (0, 0)),
                  (w_ada, (w_ada.shape[0], LANES), lambda i, j: (0, col0 // LANES + blk(i, j))),
                  (b_ada, (1, LANES), lambda i, j: (0, col0 // LANES + blk(i, j)))]
        extra_outs = [((c.shape[0], n_rest), F32, (c.shape[0], LANES), lambda i, j: (0, blk(i, j)))]

    def body(a, w, e, o):
        hv = a[0][...]
        gate = _dot(hv, w[0][...].astype(BF16))
        up = _dot(hv, w[1][...].astype(BF16))
        o[0][...] = (gate * jax.nn.sigmoid(gate) * up).astype(BF16)
        if ada is not None:
            o[1][...] = _dot(e[0][...].astype(BF16), e[1][...].astype(BF16)) + e[2][...]

    outs = _mm_call("ffn_in", m, f, bm, bn, [h], [(w_in, 0), (w_in, nj)], extras, [BF16], body, vmem_mib=56,
                    lhs_buffers=1, extra_outs=extra_outs, side_casts=side_casts)
    if ada is not None:
        outs = [outs[0]] + outs[2:] + [outs[1]]
    elif ada_separate is not None:
        outs = outs + [ada_separate]
    return outs


def _mm_res_call(name, a, w_bf, x, gate, coef, per, rows_per_group):
    m, k = a.shape
    n = w_bf.shape[1]
    bm = _pick(m, (1024, 512, 256, 128, 64, 32, 8) if k <= 4096 else (512, 256, 128, 64, 32, 8))
    bm = min(bm, rows_per_group)
    bn = _pick(n, (1024, 512, 256, 128) if k <= 4096 else (512, 256, 128))

    def body(lhs, rhs, e, o):
        acc = _dot(lhs[0][...], rhs[0][...])
        r = _row_periodic(acc, per, lambda a3: a3 * (coef * (1.0 + e[1][...])))
        o[0][...] = e[0][...] + r

    extras = [(x, (bm, bn), lambda i, j: (i, j)),
              (gate, (1, per, bn), lambda i, j: (i * bm // rows_per_group, 0, j))]
    (out,) = _mm_call(name, m, n, bm, bn, [a], [(w_bf, 0)], extras, [F32], body, vmem_mib=56)
    return out


def _head_rms(acc, gain, head_dim):
    outs = []
    for hh in range(acc.shape[1] // head_dim):
        y = acc[:, hh * head_dim:(hh + 1) * head_dim]
        ms = jnp.mean(y * y, axis=-1, keepdims=True)
        outs.append(y * lax.rsqrt(ms + EPS) * gain)
    return outs


def _log_sigmoid(z):
    return jnp.minimum(z, 0.0) - jnp.log1p(jnp.exp(-jnp.abs(z)))


def _proj_call(name, h, w_bf, col_off, n, mode, gain=None, head_dim=LANES, q_scale=1.0, w_f=None, b_f=None,
               side_casts=()):
    m, k = h.shape
    bm = _pick(m, (1024, 512, 256, 128, 64, 32, 8))
    bn = _pick(n, (1024, 512, 256, 128))
    extras, extra_outs = [], []
    if mode in ("q", "k"):
        extras.append((gain.reshape(1, head_dim), (1, head_dim), lambda i, j: (0, 0)))
    if mode == "v":
        extras.append((w_f, w_f.shape, lambda i, j: (0, 0)))
        extras.append((b_f, b_f.shape, lambda i, j: (0, 0)))
        extra_outs.append(((m, w_f.shape[1]), F32, (bm, w_f.shape[1]), lambda i, j: (i, 0)))
    out_dtypes = {"q": [BF16], "k": [F32, BF16], "v": [F32, BF16], "u": [F32]}[mode]

    def body(a, w, e, o):
        acc = _dot(a[0][...], w[0][...])
        if mode in ("q", "k"):
            parts = _head_rms(acc, e[0][...], head_dim)
            for hh, y in enumerate(parts):
                sl = slice(hh * head_dim, (hh + 1) * head_dim)
                if mode == "q":
                    o[0][:, sl] = (y * q_scale).astype(BF16)
                else:
                    o[0][:, sl] = y
                    o[1][:, sl] = y.astype(BF16)
        elif mode == "v":
            o[0][...] = acc
            o[1][...] = acc.astype(BF16)

            @pl.when(pl.program_id(1) == 0)
            def _():
                o[2][...] = _log_sigmoid(_dot(a[0][...], e[0][...]) + e[1][...])
        else:
            o[0][...] = acc

    assert col_off % bn == 0
    return _mm_call(name, m, n, bm, bn, [h], [(w_bf, col_off // bn)], extras, out_dtypes, body, vmem_mib=56,
                    extra_outs=extra_outs, side_casts=side_casts)


def _cumsum_call(lt):
    nb, nh, lp = lt.shape
    ck = CUMSUM_CHUNK
    assert lp % ck == 0
    rows = nb * nh
    rb = _pick(rows, (256, 128, 64, 32, 16, 8))

    def kern(x_ref, o_ref):
        row = lax.broadcasted_iota(jnp.int32, (ck, ck), 0)
        col = lax.broadcasted_iota(jnp.int32, (ck, ck), 1)
        upper = (row <= col).astype(F32)
        carry = jnp.zeros((rb, 1), F32)
        for c in range(lp // ck):
            xs = x_ref[:, c * ck:(c + 1) * ck]
            ys = jnp.dot(xs, upper, preferred_element_type=F32, precision=lax.Precision.HIGHEST) + carry
            o_ref[:, c * ck:(c + 1) * ck] = ys
            carry = ys[:, ck - 1:ck]

    out = pl.pallas_call(
        kern,
        out_shape=jax.ShapeDtypeStruct((rows, lp), F32),
        grid=(rows // rb,),
        in_specs=[pl.BlockSpec((rb, lp), lambda r: (r, 0))],
        out_specs=pl.BlockSpec((rb, lp), lambda r: (r, 0)),
        compiler_params=_params(1, 40),
        name="logf_cumsum",
    )(lt.reshape(rows, lp))
    return out.reshape(nb, nh, lp)


def _softmax_pv(parts, fq):
    m_keys = None
    for s, _ in parts:
        mx = jnp.max(s, axis=1, keepdims=True)
        m_keys = mx if m_keys is None else jnp.maximum(m_keys, mx)
    shift = fq - (m_keys + fq)
    l_sum, acc = None, None
    for s, v in parts:
        p = jnp.exp2(s + shift)
        ls = jnp.sum(p, axis=1, keepdims=True)
        pv = _dot(p.astype(BF16), v)
        l_sum = ls if l_sum is None else l_sum + ls
        acc = pv if acc is None else acc + pv
    return acc / l_sum


def _causal_mask(s):
    r = lax.broadcasted_iota(jnp.int32, s.shape, 0)
    c = lax.broadcasted_iota(jnp.int32, s.shape, 1)
    return jnp.where(c <= r, s, NEG_BIG)


def _attn_prompt_call(q, k, v, f_col, f_row, nb, seq, nh, head_dim):
    tq = _pick(seq, (512, 256, 128, 64, 32, 8))
    nq = seq // tq
    hp = 2 if nh % 2 == 0 else 1
    hw = hp * head_dim

    def kern(q_ref, k_ref, v_ref, fc_ref, fr_ref, o_ref, fq_ref):
        fall = fc_ref[...]
        lane = lax.broadcasted_iota(jnp.int32, fall.shape, 1)
        for j in range(hp):
            hh = pl.program_id(1) * hp + j
            fq_ref[j] = jnp.sum(jnp.where(lane == hh, fall, 0.0), axis=1, keepdims=True) * LOG2E
        for qi in range(nq):
            lo, hi = qi * tq, (qi + 1) * tq
            for j in range(hp):
                cs = slice(j * head_dim, (j + 1) * head_dim)
                qv = q_ref[lo:hi, cs]
                s_d = _causal_mask(_dot_nt(qv, k_ref[lo:hi, cs]) - fr_ref[0, j, :, lo:hi] * LOG2E)
                parts = [(s_d, v_ref[lo:hi, cs])]
                if qi > 0:
                    s_o = _dot_nt(qv, k_ref[0:lo, cs]) - fr_ref[0, j, :, 0:lo] * LOG2E
                    parts.append((s_o, v_ref[0:lo, cs]))
                o_ref[lo:hi, cs] = _softmax_pv(parts, fq_ref[j, lo:hi, :]).astype(BF16)

    return pl.pallas_call(
        kern,
        out_shape=jax.ShapeDtypeStruct((nb * seq, nh * head_dim), BF16),
        grid=(nb, nh // hp),
        in_specs=[pl.BlockSpec((seq, hw), lambda b, h: (b, h)),
                  pl.BlockSpec((seq, hw), lambda b, h: (b, h)),
                  pl.BlockSpec((seq, hw), lambda b, h: (b, h)),
                  pl.BlockSpec((seq, nh), lambda b, h: (b, 0)),
                  pl.BlockSpec((1, hp, 1, seq), lambda b, h: (b, h, 0, 0))],
        out_specs=pl.BlockSpec((seq, hw), lambda b, h: (b, h)),
        scratch_shapes=[pltpu.VMEM((hp, seq, 1), F32)],
        compiler_params=_params(2, 40),
        name="attn_prompt",
    )(q, k, v, f_col, f_row)


def _attn_sample_call(q, k_new, v_new, cache_k, cache_v, fq_col, f_cache_row, f_new_row,
                      nb, s_len, past, nh, head_dim):
    n_steps = nb * nh
    n_slots = 3
    n_parts = 2 if past % (2 * SUBLANES) == 0 else 1
    part = past // n_parts

    def kern(q_ref, kn_ref, vn_ref, fq_ref, fc_ref, fn_ref, ck_hbm, cv_hbm, o_ref, kbuf, vbuf, sem):
        step = pl.program_id(0) * nh + pl.program_id(1)
        slot = lax.rem(step, n_slots)

        def copies(s_idx, slot_idx):
            bb = s_idx // nh
            hh = lax.rem(s_idx, nh)
            out = []
            for j in range(n_parts):
                rs = pl.ds(j * part, part)
                out.append(pltpu.make_async_copy(ck_hbm.at[bb, rs, hh, :], kbuf.at[slot_idx, rs],
                                                 sem.at[0, slot_idx, j]))
                out.append(pltpu.make_async_copy(cv_hbm.at[bb, rs, hh, :], vbuf.at[slot_idx, rs],
                                                 sem.at[1, slot_idx, j]))
            return out

        @pl.when(step == 0)
        def _():
            for ahead in range(min(n_slots - 1, n_steps)):
                for cp in copies(step + ahead, ahead):
                    cp.start()

        @pl.when(step + (n_slots - 1) < n_steps)
        def _():
            for cp in copies(step + (n_slots - 1), lax.rem(step + (n_slots - 1), n_slots)):
                cp.start()

        for cp in copies(step, slot):
            cp.wait()

        qv = q_ref[...]
        s_c = _dot_nt(qv, kbuf[slot].astype(BF16)) - fc_ref[0, 0] * LOG2E
        s_n = _causal_mask(_dot_nt(qv, kn_ref[...]) - fn_ref[0, 0] * LOG2E)
        parts = [(s_c, vbuf[slot].astype(BF16)), (s_n, vn_ref[...])]
        o_ref[...] = _softmax_pv(parts, fq_ref[0, 0] * LOG2E).astype(BF16)

    return pl.pallas_call(
        kern,
        out_shape=jax.ShapeDtypeStruct((nb * s_len, nh * head_dim), BF16),
        grid=(nb, nh),
        in_specs=[pl.BlockSpec((s_len, head_dim), lambda b, h: (b, h)),
                  pl.BlockSpec((s_len, head_dim), lambda b, h: (b, h)),
                  pl.BlockSpec((s_len, head_dim), lambda b, h: (b, h)),
                  pl.BlockSpec((1, 1, s_len, 1), lambda b, h: (b, h, 0, 0)),
                  pl.BlockSpec((1, 1, 1, past), lambda b, h: (b, h, 0, 0)),
                  pl.BlockSpec((1, 1, 1, s_len), lambda b, h: (b, h, 0, 0)),
                  pl.BlockSpec(memory_space=pl.ANY),
                  pl.BlockSpec(memory_space=pl.ANY)],
        out_specs=pl.BlockSpec((s_len, head_dim), lambda b, h: (b, h)),
        scratch_shapes=[pltpu.VMEM((n_slots, past, head_dim), F32),
                        pltpu.VMEM((n_slots, past, head_dim), F32),
                        pltpu.SemaphoreType.DMA((2, n_slots, n_parts))],
        compiler_params=_params(2, 40),
        name="attn_sample",
    )(q, k_new, v_new, fq_col, f_cache_row, f_new_row, cache_k, cache_v)


def _s5_call(name, u, wb, wc, a_pack, d_row, h0_re, h0_im, nb, t_len, batch_major=False):
    rows, d_ssm = u.shape
    n_gt = wb.shape[0]
    cw = wb.shape[1]
    sw = wb.shape[2] // 2
    nt = 4 if n_gt % 4 == 0 else 1
    tc = _pick(t_len, tuple(c for c in (128, 64, 32, 16, 8, 4, 2, 1) if c * nb <= 1024))
    n_chunks = t_len // tc
    rc = tc * nb
    assert nb % SUBLANES == 0
    assert not batch_major or nb == SUBLANES
    n_steps = (n_gt // nt) * n_chunks
    uw = nt * cw

    def kern(u_ref, wb_ref, wc_ref, a_ref, d_ref, h0r_ref, h0i_ref, y_ref, hr_out, hi_out,
             hr_ref, hi_ref, bu_ref, *gather_scratch):
        c = pl.program_id(1)
        if batch_major:
            ubuf, sem = gather_scratch
            step = pl.program_id(0) * n_chunks + c
            slot = lax.rem(step, 2)

            def copies(s_idx, slot_idx):
                t0 = pl.multiple_of(lax.rem(s_idx, n_chunks) * tc, tc)
                c0 = pl.multiple_of((s_idx // n_chunks) * uw, uw)
                return [pltpu.make_async_copy(u_ref.at[b, pl.ds(t0, tc), pl.ds(c0, uw)],
                                              ubuf.at[slot_idx, :, b, :], sem.at[slot_idx, b])
                        for b in range(nb)]

            @pl.when(step == 0)
            def _():
                for cp in copies(step, slot):
                    cp.start()

            @pl.when(step + 1 < n_steps)
            def _():
                for cp in copies(step + 1, 1 - slot):
                    cp.start()

            for cp in copies(step, slot):
                cp.wait()
            u_all = ubuf[slot].reshape(rc, uw)
            u_tile = lambda g: u_all[:, g * cw:(g + 1) * cw]
        else:
            u_tile = lambda g: u_ref[:, g * cw:(g + 1) * cw]

        @pl.when(c == 0)
        def _():
            hr_ref[...] = h0r_ref[...]
            hi_ref[...] = h0i_ref[...]

        for g in range(nt):
            bu_ref[g] = _dot(u_tile(g).astype(BF16), wb_ref[g])
        for g in range(nt):
            ar = jnp.broadcast_to(a_ref[g, 0:1, :], (SUBLANES, sw))
            ai = jnp.broadcast_to(a_ref[g, 1:2, :], (SUBLANES, sw))
            for r in range(nb // SUBLANES):
                rs = slice(r * SUBLANES, (r + 1) * SUBLANES)
                hr = hr_ref[rs, g * sw:(g + 1) * sw]
                hi = hi_ref[rs, g * sw:(g + 1) * sw]
                for t in range(tc):
                    ts = slice(t * nb + r * SUBLANES, t * nb + (r + 1) * SUBLANES)
                    nhr = ar * hr - ai * hi + bu_ref[g, ts, 0:sw]
                    nhi = ar * hi + ai * hr + bu_ref[g, ts, sw:2 * sw]
                    bu_ref[g, ts, 0:sw] = nhr
                    bu_ref[g, ts, sw:2 * sw] = nhi
                    hr, hi = nhr, nhi
                hr_ref[rs, g * sw:(g + 1) * sw] = hr
                hi_ref[rs, g * sw:(g + 1) * sw] = hi
            cs = slice(g * cw, (g + 1) * cw)
            y = _dot(bu_ref[g].astype(BF16), wc_ref[g]) + d_ref[:, cs] * u_tile(g)
            y_ref[:, cs] = jax.nn.gelu(y, approximate=True).astype(BF16)

        @pl.when(c == n_chunks - 1)
        def _():
            hr_out[...] = hr_ref[...]
            hi_out[...] = hi_ref[...]

    return pl.pallas_call(
        kern,
        out_shape=[jax.ShapeDtypeStruct((rows, d_ssm), BF16),
                   jax.ShapeDtypeStruct((nb, n_gt * sw), F32),
                   jax.ShapeDtypeStruct((nb, n_gt * sw), F32)],
        grid=(n_gt // nt, n_chunks),
        in_specs=[pl.BlockSpec(memory_space=pl.ANY) if batch_major else
                  pl.BlockSpec((rc, nt * cw), lambda g, c: (c, g)),
                  pl.BlockSpec((nt, cw, 2 * sw), lambda g, c: (g, 0, 0)),
                  pl.BlockSpec((nt, 2 * sw, cw), lambda g, c: (g, 0, 0)),
                  pl.BlockSpec((nt, 2, sw), lambda g, c: (g, 0, 0)),
                  pl.BlockSpec((1, nt * cw), lambda g, c: (0, g)),
                  pl.BlockSpec((nb, nt * sw), lambda g, c: (0, g)),
                  pl.BlockSpec((nb, nt * sw), lambda g, c: (0, g))],
        out_specs=[pl.BlockSpec((rc, nt * cw), lambda g, c: (c, g)),
                   pl.BlockSpec((nb, nt * sw), lambda g, c: (0, g)),
                   pl.BlockSpec((nb, nt * sw), lambda g, c: (0, g))],
        scratch_shapes=[pltpu.VMEM((nb, nt * sw), F32),
                        pltpu.VMEM((nb, nt * sw), F32),
                        pltpu.VMEM((nt, rc, 2 * sw), F32)] + (
            [pltpu.VMEM((2, tc, nb, uw), F32), pltpu.SemaphoreType.DMA((2, nb))] if batch_major else []),
        compiler_params=_params(2, 40),
        name=name,
    )(u.reshape(nb, t_len, d_ssm) if batch_major else u, wb, wc, a_pack, d_row, h0_re, h0_im)


def _s5_discretize(a_re, a_im, log_dt, b_re, b_im, c_re, c_im):
    g, p = a_re.shape
    ch = b_re.shape[2]
    gpt = S5_GROUPS_PER_TILE
    n_gt = g // gpt
    dt = jnp.exp(log_dt.astype(F32))[:, None]
    mag = jnp.exp(a_re * dt)
    abar_re = mag * jnp.cos(a_im * dt)
    abar_im = mag * jnp.sin(a_im * dt)
    xr = abar_re - 1
    den = a_re * a_re + a_im * a_im
    coef_re = ((xr * a_re + abar_im * a_im) / den)[..., None]
    coef_im = ((abar_im * a_re - xr * a_im) / den)[..., None]
    bb_re = coef_re * b_re - coef_im * b_im
    bb_im = coef_re * b_im + coef_im * b_re
    eye = jnp.eye(gpt, dtype=F32)

    def blockdiag_in(bb):
        t = bb.reshape(n_gt, gpt, p, ch)
        return jnp.einsum("tgpc,gh->tgchp", t, eye).reshape(n_gt, gpt * ch, gpt * p)

    def blockdiag_out(cc):
        t = cc.reshape(n_gt, gpt, ch, p)
        return jnp.einsum("tgcp,gh->tgphc", t, eye).reshape(n_gt, gpt * p, gpt * ch)

    wb = jnp.concatenate([blockdiag_in(bb_re), blockdiag_in(bb_im)], axis=2).astype(BF16)
    wc = jnp.concatenate([blockdiag_out(c_re), -blockdiag_out(c_im)], axis=1).astype(BF16)
    a_pack = jnp.stack([abar_re.reshape(n_gt, gpt * p), abar_im.reshape(n_gt, gpt * p)], axis=1)
    return wb, wc, a_pack


def _merge_call(h, o_attn, y_ssm, w_gate_bf, b_gate, w_attn_out_bf, w_glu_bf):
    m, d = h.shape
    bm = _pick(m, (512, 256, 128, 64, 32, 8))
    bn = _pick(d, (512, 256, 128))
    nblk = d // bn

    def body(a, w, e, o):
        hv = a[0][...]
        ga = jax.nn.sigmoid(_dot(hv, w[0][...]) + e[0][...])
        gs = jax.nn.sigmoid(_dot(hv, w[1][...]) + e[1][...])
        ab = _dot(a[1][...], w[2][...])
        yv = a[2][...]
        sb = _dot(yv, w[3][...]) * jax.nn.sigmoid(_dot(yv, w[4][...]))
        o[0][...] = (ga * ab + gs * sb).astype(BF16)

    bg = b_gate.reshape(1, 2 * d)
    extras = [(bg, (1, bn), lambda i, j: (0, j)),
              (bg, (1, bn), lambda i, j: (0, j + nblk))]
    rhs = [(w_gate_bf, 0), (w_gate_bf, nblk), (w_attn_out_bf, 0), (w_glu_bf, 0), (w_glu_bf, nblk)]
    (merged,) = _mm_call("gated_merge", m, d, bm, bn, [h, o_attn, y_ssm], rhs, extras, [BF16], body,
                         vmem_mib=56)
    return merged


def _pad_time(x, lp):
    return jnp.pad(x, [(0, 0)] * (x.ndim - 1) + [(0, lp - x.shape[-1])])


def _layer(x, mod_rows, shared, dims, cache=None, ssm_h0=None):
    nb, t_len, d = x.shape
    nh, head_dim, n_groups, n_state = dims
    d_attn = nh * head_dim
    sample = cache is not None
    m = nb * t_len
    wts = shared
    raw = shared.pop("raw", None)
    ada = shared.pop("ada", None)
    if sample:
        xt = x.transpose(1, 0, 2).reshape(m, d)
        per, rpg = nb, m
        modg = lambda j: shared["mod"][mod_rows, j][None]
    else:
        xt = x.reshape(m, d)
        per, rpg = SUBLANES, t_len
        modg = lambda j: jnp.broadcast_to(shared["mod"][mod_rows, j][:, None, :], (nb, per, d))
    side = (lambda *names: [raw[k] for k in names]) if raw is not None else (lambda *names: [])

    h = _norm_mod_call(xt, wts["norm_g"][0], modg(1), modg(0), per, rpg)
    outs = _ffn_in_call(h, wts["ffn1_in"], side_casts=side("ffn1_out", "w_gate", "w_glu", "w_attn_out"), ada=ada)
    act = outs[0]
    if raw is not None:
        wts["ffn1_out"], wts["w_gate"], wts["w_glu"], wts["w_attn_out"] = outs[1], outs[2], outs[3], outs[4]
        w_in_bf = raw["w_in"].astype(BF16)
        wts["w_in"] = w_in_bf
        wts["w_f"] = jnp.pad(w_in_bf[:, 3 * d_attn:3 * d_attn + nh], ((0, 0), (0, LANES - nh)))
        wts["w_u"] = w_in_bf[:, 3 * d_attn + nh:]
    if ada is not None:
        n_all = shared["mod"].shape[0]
        shared["mod"] = jnp.concatenate([shared["mod"].reshape(n_all, -1), outs[-1]], axis=1).reshape(n_all, N_MOD, d)
    xt = _mm_res_call("ffn1_out", act, wts["ffn1_out"], xt, modg(2), HALF_STEP, per, rpg)

    h = _norm_mod_call(xt, wts["norm_g"][1], modg(4), modg(3), per, rpg)
    w_in_bf = wts["w_in"]
    outs = _proj_call("proj_q", h, w_in_bf, 0, d_attn, "q", gain=wts["q_norm_g"], head_dim=head_dim,
                      q_scale=head_dim ** -0.5 * LOG2E)
    q = outs[0]
    outs = _proj_call("proj_k", h, w_in_bf, d_attn, d_attn, "k", gain=wts["k_norm_g"], head_dim=head_dim)
    k32, kbf = outs[0], outs[1]
    outs = _proj_call("proj_v", h, w_in_bf, 2 * d_attn, d_attn, "v", w_f=wts["w_f"], b_f=wts["b_f"])
    v32, vbf, lf = outs[0], outs[1], outs[2]
    outs = _proj_call("proj_u", h, wts["w_u"], 0, wts["w_u"].shape[1], "u", side_casts=side("w_out"))
    u = outs[0]
    if raw is not None:
        wts["w_out"] = outs[1]
    logf = lf[:, :nh]

    ck = CUMSUM_CHUNK
    if sample:
        cache_k, cache_v, cache_logf = cache
        past = cache_k.shape[1]
        logf_bt = logf.reshape(t_len, nb, nh).transpose(1, 0, 2)
        lt = jnp.concatenate([cache_logf.astype(F32), logf_bt], axis=1).transpose(0, 2, 1)
        lp = -(-(past + t_len) // ck) * ck
        f_row = _cumsum_call(_pad_time(lt, lp))
        f_cache_row = f_row[:, :, None, :past]
        f_new_row = f_row[:, :, None, past:past + t_len]
        fq_col = f_row[:, :, past:past + t_len, None]
        to_bt = lambda a: a.reshape(t_len, nb, -1).transpose(1, 0, 2).reshape(m, -1)
        o_bt = _attn_sample_call(to_bt(q), to_bt(kbf), to_bt(vbf), cache_k, cache_v,
                                 fq_col, f_cache_row, f_new_row, nb, t_len, past, nh, head_dim)
        o_attn = o_bt.reshape(nb, t_len, d_attn).transpose(1, 0, 2).reshape(m, d_attn)
        k_out = to_bt(k32).reshape(nb, t_len, nh, head_dim)
        v_out = to_bt(v32).reshape(nb, t_len, nh, head_dim)
        logf_out = logf_bt
        u_tm = u
        h0_re = ssm_h0[0].astype(F32).reshape(nb, n_groups * n_state)
        h0_im = ssm_h0[1].astype(F32).reshape(nb, n_groups * n_state)
    else:
        lt = logf.reshape(nb, t_len, nh).transpose(0, 2, 1)
        lp = -(-t_len // ck) * ck
        f_row = _cumsum_call(_pad_time(lt, lp))[:, :, :t_len]
        f_col = f_row.transpose(0, 2, 1).reshape(m, nh)
        o_attn = _attn_prompt_call(q, kbf, vbf, f_col, f_row[:, :, None, :], nb, t_len, nh, head_dim)
        k_out = k32.reshape(nb, t_len, nh, head_dim)
        v_out = v32.reshape(nb, t_len, nh, head_dim)
        logf_out = logf.reshape(nb, t_len, nh)
        gather = nb == SUBLANES
        u_tm = u if gather else u.reshape(nb, t_len, -1).transpose(1, 0, 2).reshape(m, -1)
        h0_re = jnp.zeros((nb, n_groups * n_state), F32)
        h0_im = h0_re

    y_tm, h_re, h_im = _s5_call("s5_sample" if sample else "s5_prompt", u_tm, wts["s5_wb"], wts["s5_wc"],
                                wts["s5_a"], wts["s5_d"], h0_re, h0_im, nb, t_len,
                                batch_major=not sample and gather)
    if sample:
        y_ssm = y_tm
    else:
        y_ssm = y_tm.reshape(t_len, nb, -1).transpose(1, 0, 2).reshape(m, -1)

    merged = _merge_call(h, o_attn, y_ssm, wts["w_gate"], wts["b_gate"], wts["w_attn_out"], wts["w_glu"])
    xt = _mm_res_call("mix_out", merged, wts["w_out"], xt, modg(5), 1.0, per, rpg)

    h = _norm_mod_call(xt, wts["norm_g"][2], modg(7), modg(6), per, rpg)
    outs = _ffn_in_call(h, wts["ffn2_in"], side_casts=side("ffn2_out"))
    act = outs[0]
    if raw is not None:
        wts["ffn2_out"] = outs[1]
    xt = _mm_res_call("ffn2_out", act, wts["ffn2_out"], xt, modg(8), HALF_STEP, per, rpg)

    if sample:
        y = xt.reshape(t_len, nb, d).transpose(1, 0, 2)
    else:
        y = xt.reshape(nb, t_len, d)
    return (y, k_out, v_out, logf_out,
            h_re.reshape(nb, n_groups, n_state), h_im.reshape(nb, n_groups, n_state))


def kernel(x_prompt, x_sample, cache_k, cache_v, cache_logf, state_ssm_re, state_ssm_im, c_prompt, c_sample, w_ada, b_ada, norm_g, w_ffn1_in, w_ffn1_out, w_in, b_forget, q_norm_g, k_norm_g, w_attn_out, ssm_a_re, ssm_a_im, ssm_log_dt, ssm_b_re, ssm_b_im, ssm_c_re, ssm_c_im, ssm_d, w_glu, w_gate, b_gate, w_out, w_ffn2_in, w_ffn2_out):
    depth = w_ada.shape[0]
    nbp = x_prompt.shape[0]
    d = x_prompt.shape[2]
    nh, head_dim = cache_k.shape[3], cache_k.shape[4]
    n_groups, n_state, n_ch = ssm_b_re.shape[1:]
    d_attn = nh * head_dim
    d_ssm = n_groups * n_ch
    dims = (nh, head_dim, n_groups, n_state)
    assert head_dim == LANES and nh <= LANES
    assert S5_GROUPS_PER_TILE * n_ch == LANES and n_groups % S5_GROUPS_PER_TILE == 0

    xp, xs = x_prompt, x_sample
    outs_p, outs_s = [], []
    for l in range(depth):
        c_all = jnp.concatenate([c_prompt, c_sample], axis=0)
        b_ada_row = b_ada[l].reshape(1, -1)
        n_first = 2 * d
        mod_first = _ada_call(c_all, w_ada[l], b_ada_row, 0, n_first).reshape(c_all.shape[0], 2, d)
        wb, wc, a_pack = _s5_discretize(ssm_a_re[l].astype(F32), ssm_a_im[l].astype(F32), ssm_log_dt[l],
                                        ssm_b_re[l].astype(F32), ssm_b_im[l].astype(F32),
                                        ssm_c_re[l].astype(F32), ssm_c_im[l].astype(F32))
        shared = {
            "mod": mod_first,
            "ada": (c_all, w_ada[l], b_ada_row, n_first),
            "raw": {"ffn1_out": w_ffn1_out[l], "ffn2_out": w_ffn2_out[l], "w_in": w_in[l],
                    "w_gate": w_gate[l], "w_glu": w_glu[l], "w_attn_out": w_attn_out[l], "w_out": w_out[l]},
            "norm_g": norm_g[l].astype(F32),
            "ffn1_in": w_ffn1_in[l], "ffn2_in": w_ffn2_in[l],
            "b_f": jnp.pad(b_forget[l].astype(F32), (0, LANES - nh)).reshape(1, LANES),
            "q_norm_g": q_norm_g[l].astype(F32), "k_norm_g": k_norm_g[l].astype(F32),
            "b_gate": b_gate[l].astype(F32),
            "s5_wb": wb, "s5_wc": wc, "s5_a": a_pack, "s5_d": ssm_d[l].astype(F32).reshape(1, d_ssm),
        }
        xp, k1, v1, f1, r1, i1 = _layer(xp, slice(0, nbp), shared, dims)
        xs, k2, v2, f2, r2, i2 = _layer(xs, slice(nbp, None), shared, dims,
                                        cache=(cache_k[l], cache_v[l], cache_logf[l]),
                                        ssm_h0=(state_ssm_re[l], state_ssm_im[l]))
        outs_p.append((k1, v1, f1, r1, i1))
        outs_s.append((k2, v2, f2, r2, i2))

    stack = lambda outs, idx: jnp.stack([o[idx] for o in outs])
    return (xp, xs,
            stack(outs_p, 0), stack(outs_p, 1), stack(outs_p, 2), stack(outs_p, 3), stack(outs_p, 4),
            stack(outs_s, 0), stack(outs_s, 1), stack(outs_s, 2), stack(outs_s, 3), stack(outs_s, 4))
```
